```python
import math
import jax, jax.numpy as jnp
from jax import lax
import numpy as np

D_MODEL = 1024
BATCH = 1
SEQ = 16384
DEPTH = 2

ATTN_HEADS = 8
ATTN_HEAD_DIM = 64
ATTN_V_DIM = 2 * ATTN_HEAD_DIM
ATTN_WIDTH = ATTN_HEADS * 2 * ATTN_HEAD_DIM
Q_BLOCK = 128
CONV_WIDTH = 512
CONV_K = 3
POOL_WINDOWS = (2, 4, 8, 16)
POOL_GROUP = 128
POOL_WIDTH = POOL_GROUP * len(POOL_WINDOWS)
N_BRANCH = 3
D_FF = -(-8 * D_MODEL // (3 * 256)) * 256
EPS = 1e-6

OFF_Q = 0
OFF_K = OFF_Q + ATTN_WIDTH
OFF_V = OFF_K + ATTN_WIDTH
OFF_CB = OFF_V + ATTN_WIDTH
OFF_CC = OFF_CB + CONV_WIDTH
OFF_CH = OFF_CC + CONV_WIDTH
OFF_POOL = OFF_CH + CONV_WIDTH
OFF_GATE = OFF_POOL + POOL_WIDTH
IN_COLS = OFF_GATE + N_BRANCH * D_MODEL

kernel_name = "hybrid_diffattn_conv_pool_gated"


def rms_norm(x, g):
    xf = x.astype(jnp.float32)
    y = xf * lax.rsqrt(jnp.mean(xf * xf, axis=-1, keepdims=True) + EPS)
    return (y * g.astype(jnp.float32)).astype(x.dtype)


def diff_attention(q, k, v, lam, subln_g, lambda_init):
    b, s = q.shape[0], q.shape[1]
    nb = s // Q_BLOCK
    scale = ATTN_HEAD_DIM ** -0.5
    qb = (q * scale).reshape(b, nb, Q_BLOCK, ATTN_HEADS, 2, ATTN_HEAD_DIM)
    qb = qb.transpose(1, 0, 3, 4, 2, 5)
    kt = k.transpose(0, 2, 3, 1, 4)
    vt = v.transpose(0, 2, 1, 3)
    k_pos = jnp.arange(s)

    def block(args):
        q_blk, i = args
        q_pos = i * Q_BLOCK + jnp.arange(Q_BLOCK)
        sc = jnp.einsum('bhcqd,bhckd->bhcqk', q_blk, kt).astype(jnp.float32)
        causal = k_pos[None, :] <= q_pos[:, None]
        sc = jnp.where(causal, sc, -jnp.inf)
        p = jax.nn.softmax(sc, axis=-1)
        p_diff = p[:, :, 0] - lam * p[:, :, 1]
        return jnp.einsum('bhqk,bhkv->bhqv', p_diff.astype(vt.dtype), vt)

    o = lax.map(block, (qb, jnp.arange(nb)))
    o = rms_norm(o, subln_g) * (1.0 - lambda_init)
    return o.transpose(1, 0, 3, 2, 4).reshape(b, s, ATTN_HEADS * ATTN_V_DIM)


def short_conv(u, w):
    s = u.shape[1]
    up = jnp.pad(u, ((0, 0), (CONV_K - 1, 0), (0, 0)))
    return sum(up[:, j:j + s] * w[j] for j in range(CONV_K))


def multiscale_pool(u, pool_w, pool_scale):
    s = u.shape[1]
    uf = u.astype(jnp.float32)
    cs = jnp.cumsum(uf, axis=1)
    t = jnp.arange(1, s + 1, dtype=jnp.float32)[None, :, None]
    outs = []
    for gi, w in enumerate(POOL_WINDOWS):
        sl = slice(gi * POOL_GROUP, (gi + 1) * POOL_GROUP)
        cg = cs[..., sl]
        prev = jnp.pad(cg, ((0, 0), (w, 0), (0, 0)))[:, :s]
        mean = (cg - prev) / jnp.minimum(t, float(w))
        d = (mean - uf[..., sl]).astype(u.dtype)
        outs.append(jnp.einsum('bsc,cd->bsd', d, pool_w[gi]))
    return jnp.concatenate(outs, axis=-1) * pool_scale


def setup_inputs(seed: int = 0) -> dict:
    key = jax.random.key(seed)
    ks = jax.random.split(key, 24)
    f32 = jnp.float32
    nrm = lambda k, shape, fan_in: jax.random.normal(k, shape, f32) * fan_in ** -0.5
    gain = lambda k, shape: 1.0 + 0.02 * jax.random.normal(k, shape, f32)
    L, D = DEPTH, D_MODEL
    return {
        "x": jax.random.normal(ks[0], (BATCH, SEQ, D), f32),
        "g_pre_mix": gain(ks[1], (L, D)),
        "w_in": nrm(ks[2], (L, D, IN_COLS), D),
        "b_gate": 0.02 * jax.random.normal(ks[3], (L, N_BRANCH * D), f32),
        "lambda_q1": 0.1 * jax.random.normal(ks[4], (L, ATTN_HEAD_DIM), f32),
        "lambda_k1": 0.1 * jax.random.normal(ks[5], (L, ATTN_HEAD_DIM), f32),
        "lambda_q2": 0.1 * jax.random.normal(ks[6], (L, ATTN_HEAD_DIM), f32),
        "lambda_k2": 0.1 * jax.random.normal(ks[7], (L, ATTN_HEAD_DIM), f32),
        "subln_g": gain(ks[8], (L, ATTN_V_DIM)),
        "w_attn_proj": nrm(ks[9], (L, ATTN_WIDTH, D), ATTN_WIDTH),
        "conv_w": nrm(ks[10], (L, CONV_K, CONV_WIDTH), CONV_K),
        "w_conv_proj": nrm(ks[11], (L, CONV_WIDTH, D), CONV_WIDTH),
        "pool_w": nrm(ks[12], (L, len(POOL_WINDOWS), POOL_GROUP, POOL_GROUP), POOL_GROUP),
        "pool_scale": 1.0 + 0.1 * jax.random.normal(ks[13], (L, POOL_WIDTH), f32),
        "w_pool_proj": nrm(ks[14], (L, POOL_WIDTH, D), POOL_WIDTH),
        "w_out": nrm(ks[15], (L, D, D), D),
        "g_post_mix": gain(ks[16], (L, D)),
        "g_pre_ffn": gain(ks[17], (L, D)),
        "w_gate_up": nrm(ks[18], (L, D, 2 * D_FF), D),
        "w_down": nrm(ks[19], (L, D_FF, D), D_FF),
        "g_post_ffn": gain(ks[20], (L, D)),
    }


def reference(x, g_pre_mix, w_in, b_gate, lambda_q1, lambda_k1, lambda_q2, lambda_k2,
              subln_g, w_attn_proj, conv_w, w_conv_proj, pool_w, pool_scale, w_pool_proj,
              w_out, g_post_mix, g_pre_ffn, w_gate_up, w_down, g_post_ffn):
    b, s, d = x.shape
    for l in range(DEPTH):
        xn = rms_norm(x, g_pre_mix[l])
        z = jnp.einsum('bsd,dc->bsc', xn, w_in[l])

        q = z[..., OFF_Q:OFF_K].reshape(b, s, ATTN_HEADS, 2, ATTN_HEAD_DIM)
        k = z[..., OFF_K:OFF_V].reshape(b, s, ATTN_HEADS, 2, ATTN_HEAD_DIM)
        v = z[..., OFF_V:OFF_CB].reshape(b, s, ATTN_HEADS, ATTN_V_DIM)
        lambda_init = 0.8 - 0.6 * math.exp(-0.3 * l)
        lam = (jnp.exp(jnp.sum(lambda_q1[l].astype(jnp.float32) * lambda_k1[l].astype(jnp.float32)))
               - jnp.exp(jnp.sum(lambda_q2[l].astype(jnp.float32) * lambda_k2[l].astype(jnp.float32)))
               + lambda_init)
        y_attn = diff_attention(q, k, v, lam, subln_g[l], lambda_init)

        gb = z[..., OFF_CB:OFF_CC]
        gc = z[..., OFF_CC:OFF_CH]
        hc = z[..., OFF_CH:OFF_POOL]
        y_conv = gb * short_conv(gc * hc, conv_w[l])

        y_pool = multiscale_pool(z[..., OFF_POOL:OFF_GATE], pool_w[l], pool_scale[l])

        gates = jax.nn.sigmoid(z[..., OFF_GATE:] + b_gate[l]).reshape(b, s, N_BRANCH, d)
        merged = (gates[:, :, 0] * jnp.einsum('bsc,cd->bsd', y_attn, w_attn_proj[l])
                  + gates[:, :, 1] * jnp.einsum('bsc,cd->bsd', y_conv, w_conv_proj[l])
                  + gates[:, :, 2] * jnp.einsum('bsc,cd->bsd', y_pool, w_pool_proj[l]))
        mix = jnp.einsum('bsd,de->bse', merged, w_out[l])
        x = x + rms_norm(mix, g_post_mix[l])

        hn = rms_norm(x, g_pre_ffn[l])
        gu = jnp.einsum('bsd,df->bsf', hn, w_gate_up[l])
        hf = jax.nn.silu(gu[..., :D_FF]) * gu[..., D_FF:]
        ff = jnp.einsum('bsf,fd->bsd', hf, w_down[l])
        x = x + rms_norm(ff, g_post_ffn[l])
    return x
```

```python
import functools
import math

import jax
import jax.numpy as jnp
from jax import lax
from jax.experimental import pallas as pl
from jax.experimental.pallas import tpu as pltpu

F32 = jnp.float32
BF16 = jnp.bfloat16

EPS = 1e-6
N_HEADS = 8
HEAD_DIM = 64
V_DIM = 2 * HEAD_DIM
ATTN_WIDTH = N_HEADS * V_DIM
CONV_WIDTH = 512
CONV_K = 3
POOL_WINDOWS = (2, 4, 8, 16)
POOL_GROUP = 128
POOL_WIDTH = POOL_GROUP * len(POOL_WINDOWS)
HALO = 16

OFF_K = ATTN_WIDTH
OFF_V = 2 * ATTN_WIDTH
OFF_CB = 3 * ATTN_WIDTH
OFF_CC = OFF_CB + CONV_WIDTH
OFF_CH = OFF_CC + CONV_WIDTH
OFF_POOL = OFF_CH + CONV_WIDTH
OFF_GATE = OFF_POOL + POOL_WIDTH

VMEM_LIMIT_BYTES = 56 * 1024 * 1024


def _rms(x, g):
    return x * lax.rsqrt(jnp.mean(x * x, axis=-1, keepdims=True) + EPS) * g


def _inproj_kernel(x_ref, g_ref, w_ref, z_ref, xn_ref):
    @pl.when(pl.program_id(1) == 0)
    def _():
        xn_ref[...] = _rms(x_ref[...], g_ref[...]).astype(BF16)

    z_ref[...] = jnp.dot(xn_ref[...], w_ref[...], preferred_element_type=F32).astype(BF16)


def _in_proj(x, g, w, *, tm, tn):
    s, d = x.shape
    n = w.shape[1]
    return pl.pallas_call(
        _inproj_kernel,
        grid=(s // tm, n // tn),
        in_specs=[
            pl.BlockSpec((tm, d), lambda i, j: (i, 0)),
            pl.BlockSpec((1, d), lambda i, j: (0, 0)),
            pl.BlockSpec((d, tn), lambda i, j: (0, j)),
        ],
        out_specs=pl.BlockSpec((tm, tn), lambda i, j: (i, j)),
        out_shape=jax.ShapeDtypeStruct((s, n), BF16),
        scratch_shapes=[pltpu.VMEM((tm, d), BF16)],
        compiler_params=pltpu.CompilerParams(
            dimension_semantics=("parallel", "arbitrary"),
            vmem_limit_bytes=VMEM_LIMIT_BYTES),
        name="in_proj",
    )(x, g, w)


def _attn_kernel(lam_ref, g_ref, q_ref, k_ref, v_ref, o_ref,
                 m_ref, l_ref, acc_ref, *, tq, lambda_init):
    qi = pl.program_id(1)
    tk = tq

    q = q_ref[...] * BF16(HEAD_DIM ** -0.5)
    lane = lax.broadcasted_iota(jnp.int32, (tq, V_DIM), 1)
    zero = jnp.zeros_like(q)
    qh = (jnp.where(lane < HEAD_DIM, q, zero), jnp.where(lane >= HEAD_DIM, q, zero))

    m_ref[...] = jnp.full(m_ref.shape, -jnp.inf, F32)
    l_ref[...] = jnp.zeros(l_ref.shape, F32)
    acc_ref[...] = jnp.zeros(acc_ref.shape, F32)

    def block(j, masked):
        start = pl.multiple_of(j * tk, tk)
        kb = k_ref[pl.ds(start, tk), :]
        vb = v_ref[pl.ds(start, tk), :]
        if masked:
            krow = lax.broadcasted_iota(jnp.int32, (tk, tq), 0)
            qcol = lax.broadcasted_iota(jnp.int32, (tk, tq), 1)
            keep = krow <= qcol
        for c in range(2):
            st = lax.dot_general(kb, qh[c], (((1,), (1,)), ((), ())),
                                 preferred_element_type=F32)
            if masked:
                st = jnp.where(keep, st, -jnp.inf)
            m_old = m_ref[c]
            m_new = jnp.maximum(m_old, jnp.max(st, axis=0, keepdims=True))
            p = jnp.exp(st - m_new)
            alpha = jnp.exp(m_old - m_new)
            l_ref[c] = alpha * l_ref[c] + jnp.sum(p, axis=0, keepdims=True)
            m_ref[c] = m_new
            pv = lax.dot_general(vb, p.astype(BF16), (((0,), (0,)), ((), ())),
                                 preferred_element_type=F32)
            acc_ref[c] = alpha * acc_ref[c] + pv

    def body(j, carry):
        block(j, masked=False)
        return carry

    lax.fori_loop(0, qi, body, 0)
    block(qi, masked=True)

    lq1, lk1, lq2, lk2 = (lam_ref[pl.ds(r, 1), :] for r in range(4))
    lam = (jnp.exp(jnp.sum(lq1 * lk1, axis=-1, keepdims=True))
           - jnp.exp(jnp.sum(lq2 * lk2, axis=-1, keepdims=True)) + lambda_init)
    o = acc_ref[0] / l_ref[0] - lam * (acc_ref[1] / l_ref[1])
    ms = jnp.mean(o * o, axis=0, keepdims=True)
    on = o * lax.rsqrt(ms + EPS) * g_ref[...] * (1.0 - lambda_init)
    o_ref[...] = on.T.astype(BF16)


def _attention(z, lam_params, subln_g, *, tq, lambda_init):
    s = z.shape[0]
    kern = functools.partial(_attn_kernel, tq=tq, lambda_init=lambda_init)
    return pl.pallas_call(
        kern,
        grid=(N_HEADS, s // tq),
        in_specs=[
            pl.BlockSpec((4, HEAD_DIM), lambda h, i: (0, 0)),
            pl.BlockSpec((V_DIM, 1), lambda h, i: (0, 0)),
            pl.BlockSpec((tq, V_DIM), lambda h, i: (i, h)),
            pl.BlockSpec((s, V_DIM), lambda h, i: (0, OFF_K // V_DIM + h)),
            pl.BlockSpec((s, V_DIM), lambda h, i: (0, OFF_V // V_DIM + h)),
        ],
        out_specs=pl.BlockSpec((tq, V_DIM), lambda h, i: (i, h)),
        out_shape=jax.ShapeDtypeStruct((s, ATTN_WIDTH), BF16),
        scratch_shapes=[
            pltpu.VMEM((2, 1, tq), F32),
            pltpu.VMEM((2, 1, tq), F32),
            pltpu.VMEM((2, V_DIM, tq), F32),
        ],
        compiler_params=pltpu.CompilerParams(
            dimension_semantics=("parallel", "arbitrary"),
            vmem_limit_bytes=VMEM_LIMIT_BYTES),
        name="diff_attn",
    )(lam_params, subln_g, z, z, z)


def _merge_kernel(x_ref, ya_ref, cb_ref, cc_ref, ch_ref, pu_ref, ga_ref, gb_ref, gc_ref,
                  cch_ref, chh_ref, puh_ref,
                  bg_ref, cw_ref, pw_ref, ps_ref, pa_ref, pb_ref, pc_ref, wo_ref, gp_ref,
                  o_ref, ext_ref, *, tm):
    i = pl.program_id(0)
    not_first = (i > 0).astype(F32)

    u = cc_ref[...].astype(F32) * ch_ref[...].astype(F32)
    ext_ref[pl.ds(0, HALO), :] = cch_ref[...].astype(F32) * chh_ref[...].astype(F32) * not_first
    ext_ref[pl.ds(HALO, tm), :] = u
    conv = u * cw_ref[pl.ds(CONV_K - 1, 1), :]
    for j in range(CONV_K - 1):
        back = CONV_K - 1 - j
        conv = conv + ext_ref[pl.ds(HALO - back, tm), :] * cw_ref[pl.ds(j, 1), :]
    y_conv = (cb_ref[...].astype(F32) * conv).astype(BF16)

    pu = pu_ref[...].astype(F32)
    ext_ref[pl.ds(0, HALO), :] = puh_ref[...].astype(F32) * not_first
    ext_ref[pl.ds(HALO, tm), :] = pu
    t1 = (i * tm + 1 + lax.broadcasted_iota(jnp.int32, (tm, 1), 0)).astype(F32)
    groups = []
    for gi, w in enumerate(POOL_WINDOWS):
        cols = pl.ds(gi * POOL_GROUP, POOL_GROUP)
        ug = pu[:, gi * POOL_GROUP:(gi + 1) * POOL_GROUP]
        win = ug
        for back in range(1, w):
            win = win + ext_ref[pl.ds(HALO - back, tm), cols]
        dgrp = win / jnp.minimum(t1, float(w)) - ug
        groups.append(jnp.dot(dgrp.astype(BF16), pw_ref[gi], preferred_element_type=F32))
    y_pool = (jnp.concatenate(groups, axis=-1) * ps_ref[...]).astype(BF16)

    bg = bg_ref[...]
    d = x_ref.shape[-1]
    merged = (jax.nn.sigmoid(ga_ref[...].astype(F32) + bg[:, 0:d])
              * jnp.dot(ya_ref[...], pa_ref[...], preferred_element_type=F32))
    merged += (jax.nn.sigmoid(gb_ref[...].astype(F32) + bg[:, d:2 * d])
               * jnp.dot(y_conv, pb_ref[...], preferred_element_type=F32))
    merged += (jax.nn.sigmoid(gc_ref[...].astype(F32) + bg[:, 2 * d:3 * d])
               * jnp.dot(y_pool, pc_ref[...], preferred_element_type=F32))
    mix = jnp.dot(merged.astype(BF16), wo_ref[...], preferred_element_type=F32)
    o_ref[...] = x_ref[...] + _rms(mix, gp_ref[...])


def _merge(x, z, ya, bg, cw, pw, ps, pa, pb, pc, wo, gp, *, tm):
    s, d = x.shape
    hb = tm // HALO

    def col(width, off):
        return pl.BlockSpec((tm, width), lambda i: (i, off // width))

    def halo(off):
        return pl.BlockSpec((HALO, CONV_WIDTH),
                            lambda i: (jnp.maximum(i * hb - 1, 0), off // CONV_WIDTH))

    def whole(a):
        return pl.BlockSpec(a.shape, lambda i: (0,) * a.ndim)

    kern = functools.partial(_merge_kernel, tm=tm)
    return pl.pallas_call(
        kern,
        grid=(s // tm,),
        in_specs=[
            pl.BlockSpec((tm, d), lambda i: (i, 0)),
            pl.BlockSpec((tm, ATTN_WIDTH), lambda i: (i, 0)),
            col(CONV_WIDTH, OFF_CB), col(CONV_WIDTH, OFF_CC), col(CONV_WIDTH, OFF_CH),
            col(POOL_WIDTH, OFF_POOL),
            col(d, OFF_GATE), col(d, OFF_GATE + d), col(d, OFF_GATE + 2 * d),
            halo(OFF_CC), halo(OFF_CH), halo(OFF_POOL),
            whole(bg), whole(cw), whole(pw), whole(ps), whole(pa), whole(pb), whole(pc),
            whole(wo), whole(gp),
        ],
        out_specs=pl.BlockSpec((tm, d), lambda i: (i, 0)),
        out_shape=jax.ShapeDtypeStruct((s, d), F32),
        scratch_shapes=[pltpu.VMEM((tm + HALO, CONV_WIDTH), F32)],
        compiler_params=pltpu.CompilerParams(
            dimension_semantics=("parallel",),
            vmem_limit_bytes=VMEM_LIMIT_BYTES),
        name="merge",
    )(x, ya, z, z, z, z, z, z, z, z, z, z, bg, cw, pw, ps, pa, pb, pc, wo, gp)


def _ffn_kernel(x_ref, gpre_ref, wg_ref, wu_ref, wd_ref, gpost_ref, o_ref, hn_ref, acc_ref):
    c = pl.program_id(1)

    @pl.when(c == 0)
    def _():
        hn_ref[...] = _rms(x_ref[...], gpre_ref[...]).astype(BF16)
        acc_ref[...] = jnp.zeros(acc_ref.shape, F32)

    hn = hn_ref[...]
    gate = jnp.dot(hn, wg_ref[...], preferred_element_type=F32)
    up = jnp.dot(hn, wu_ref[...], preferred_element_type=F32)
    hf = (gate * jax.nn.sigmoid(gate) * up).astype(BF16)
    acc_ref[...] += jnp.dot(hf, wd_ref[...], preferred_element_type=F32)

    @pl.when(c == pl.num_programs(1) - 1)
    def _():
        o_ref[...] = x_ref[...] + _rms(acc_ref[...], gpost_ref[...])


def _ffn(x, gpre, wgu, wd, gpost, *, tm, fc):
    s, d = x.shape
    d_ff = wd.shape[0]
    nc = d_ff // fc
    return pl.pallas_call(
        _ffn_kernel,
        grid=(s // tm, nc),
        in_specs=[
            pl.BlockSpec((tm, d), lambda i, c: (i, 0)),
            pl.BlockSpec((1, d), lambda i, c: (0, 0)),
            pl.BlockSpec((d, fc), lambda i, c: (0, c)),
            pl.BlockSpec((d, fc), lambda i, c: (0, nc + c)),
            pl.BlockSpec((fc, d), lambda i, c: (c, 0)),
            pl.BlockSpec((1, d), lambda i, c: (0, 0)),
        ],
        out_specs=pl.BlockSpec((tm, d), lambda i, c: (i, 0)),
        out_shape=jax.ShapeDtypeStruct((s, d), F32),
        scratch_shapes=[pltpu.VMEM((tm, d), BF16), pltpu.VMEM((tm, d), F32)],
        compiler_params=pltpu.CompilerParams(
            dimension_semantics=("parallel", "arbitrary"),
            vmem_limit_bytes=VMEM_LIMIT_BYTES),
        name="ffn",
    )(x, gpre, wgu, wgu, wd, gpost)


def _tiles(s):
    pick = lambda pref: min(pref, s)
    return dict(proj_tm=pick(1024), proj_tn=2048, tq=pick(256), merge_tm=pick(512), ffn_tm=pick(1024))


def kernel(x, g_pre_mix, w_in, b_gate, lambda_q1, lambda_k1, lambda_q2, lambda_k2, subln_g,
           w_attn_proj, conv_w, w_conv_proj, pool_w, pool_scale, w_pool_proj, w_out,
           g_post_mix, g_pre_ffn, w_gate_up, w_down, g_post_ffn):
    b, s, d = x.shape
    depth = w_in.shape[0]
    d_ff = w_down.shape[1]
    t = _tiles(s)
    row = lambda a: a.reshape(1, -1)
    outs = []
    for bi in range(b):
        h = x[bi]
        for l in range(depth):
            lambda_init = 0.8 - 0.6 * math.exp(-0.3 * l)
            z = _in_proj(h, row(g_pre_mix[l]), w_in[l].astype(BF16), tm=t["proj_tm"], tn=t["proj_tn"])
            lam_params = jnp.stack([lambda_q1[l], lambda_k1[l], lambda_q2[l], lambda_k2[l]]).astype(F32)
            ya = _attention(z, lam_params, subln_g[l].reshape(V_DIM, 1), tq=t["tq"],
                            lambda_init=lambda_init)
            h = _merge(h, z, ya, row(b_gate[l]), conv_w[l], pool_w[l].astype(BF16), row(pool_scale[l]),
                       w_attn_proj[l].astype(BF16), w_conv_proj[l].astype(BF16),
                       w_pool_proj[l].astype(BF16), w_out[l].astype(BF16), row(g_post_mix[l]),
                       tm=t["merge_tm"])
            h = _ffn(h, row(g_pre_ffn[l]), w_gate_up[l].astype(BF16), w_down[l].astype(BF16),
                     row(g_post_ffn[l]), tm=t["ffn_tm"], fc=d_ff // 2)
        outs.append(h)
    return jnp.stack(outs)
```

```python
import functools
import math

import jax
import jax.numpy as jnp
from jax import lax
from jax.experimental import pallas as pl
from jax.experimental.pallas import tpu as pltpu

F32 = jnp.float32
BF16 = jnp.bfloat16

EPS = 1e-6
N_HEADS = 8
HEAD_DIM = 64
V_DIM = 2 * HEAD_DIM
ATTN_WIDTH = N_HEADS * V_DIM
CONV_WIDTH = 512
CONV_K = 3
POOL_WINDOWS = (2, 4, 8, 16)
POOL_GROUP = 128
POOL_WIDTH = POOL_GROUP * len(POOL_WINDOWS)
HALO = 16

OFF_K = ATTN_WIDTH
OFF_V = 2 * ATTN_WIDTH
OFF_CB = 3 * ATTN_WIDTH
OFF_CC = OFF_CB + CONV_WIDTH
OFF_CH = OFF_CC + CONV_WIDTH
OFF_POOL = OFF_CH + CONV_WIDTH
OFF_GATE = OFF_POOL + POOL_WIDTH

VMEM_LIMIT_BYTES = 56 * 1024 * 1024


def _rms(x, g):
    return x * lax.rsqrt(jnp.mean(x * x, axis=-1, keepdims=True) + EPS) * g


def _inproj_kernel(x_ref, g_ref, w_ref, z_ref, xn_ref):
    @pl.when(pl.program_id(1) == 0)
    def _():
        xn_ref[...] = _rms(x_ref[...], g_ref[...]).astype(BF16)

    z_ref[...] = jnp.dot(xn_ref[...], w_ref[...], preferred_element_type=F32).astype(BF16)


def _in_proj(x, g, w, *, tm, tn):
    s, d = x.shape
    n = w.shape[1]
    return pl.pallas_call(
        _inproj_kernel,
        grid=(s // tm, n // tn),
        in_specs=[
            pl.BlockSpec((tm, d), lambda i, j: (i, 0)),
            pl.BlockSpec((1, d), lambda i, j: (0, 0)),
            pl.BlockSpec((d, tn), lambda i, j: (0, j)),
        ],
        out_specs=pl.BlockSpec((tm, tn), lambda i, j: (i, j)),
        out_shape=jax.ShapeDtypeStruct((s, n), BF16),
        scratch_shapes=[pltpu.VMEM((tm, d), BF16)],
        compiler_params=pltpu.CompilerParams(
            dimension_semantics=("parallel", "arbitrary"),
            vmem_limit_bytes=VMEM_LIMIT_BYTES),
        name="in_proj",
    )(x, g, w)


def _attn_kernel(lam_ref, g_ref, q_ref, k_ref, v_ref, o_ref,
                 m_ref, l_ref, acc_ref, *, tq, lambda_init):
    qi = pl.program_id(1)
    tk = tq

    q = q_ref[...] * BF16(HEAD_DIM ** -0.5)
    lane = lax.broadcasted_iota(jnp.int32, (tq, V_DIM), 1)
    zero = jnp.zeros_like(q)
    qh = (jnp.where(lane < HEAD_DIM, q, zero), jnp.where(lane >= HEAD_DIM, q, zero))

    m_ref[...] = jnp.full(m_ref.shape, -jnp.inf, F32)
    l_ref[...] = jnp.zeros(l_ref.shape, F32)
    acc_ref[...] = jnp.zeros(acc_ref.shape, F32)

    def block(j, masked):
        start = pl.multiple_of(j * tk, tk)
        kb = k_ref[pl.ds(start, tk), :]
        vb = v_ref[pl.ds(start, tk), :]
        if masked:
            krow = lax.broadcasted_iota(jnp.int32, (tk, tq), 0)
            qcol = lax.broadcasted_iota(jnp.int32, (tk, tq), 1)
            keep = krow <= qcol
        for c in range(2):
            st = lax.dot_general(kb, qh[c], (((1,), (1,)), ((), ())),
                                 preferred_element_type=F32)
            if masked:
                st = jnp.where(keep, st, -jnp.inf)
            m_old = m_ref[c]
            m_new = jnp.maximum(m_old, jnp.max(st, axis=0, keepdims=True))
            p = jnp.exp(st - m_new)
            alpha = jnp.exp(m_old - m_new)
            l_ref[c] = alpha * l_ref[c] + jnp.sum(p, axis=0, keepdims=True)
            m_ref[c] = m_new
            pv = lax.dot_general(vb, p.astype(BF16), (((0,), (0,)), ((), ())),
                                 preferred_element_type=F32)
            acc_ref[c] = alpha * acc_ref[c] + pv

    def body(j, carry):
        block(j, masked=False)
        return carry

    lax.fori_loop(0, qi, body, 0)
    block(qi, masked=True)

    lq1, lk1, lq2, lk2 = (lam_ref[pl.ds(r, 1), :] for r in range(4))
    lam = (jnp.exp(jnp.sum(lq1 * lk1, axis=-1, keepdims=True))
           - jnp.exp(jnp.sum(lq2 * lk2, axis=-1, keepdims=True)) + lambda_init)
    o = acc_ref[0] / l_ref[0] - lam * (acc_ref[1] / l_ref[1])
    ms = jnp.mean(o * o, axis=0, keepdims=True)
    on = o * lax.rsqrt(ms + EPS) * g_ref[...] * (1.0 - lambda_init)
    o_ref[...] = on.T.astype(BF16)


def _attention(z, lam_params, subln_g, *, tq, lambda_init):
    s = z.shape[0]
    kern = functools.partial(_attn_kernel, tq=tq, lambda_init=lambda_init)
    return pl.pallas_call(
        kern,
        grid=(N_HEADS, s // tq),
        in_specs=[
            pl.BlockSpec((4, HEAD_DIM), lambda h, i: (0, 0)),
            pl.BlockSpec((V_DIM, 1), lambda h, i: (0, 0)),
            pl.BlockSpec((tq, V_DIM), lambda h, i: (i, h)),
            pl.BlockSpec((s, V_DIM), lambda h, i: (0, OFF_K // V_DIM + h)),
            pl.BlockSpec((s, V_DIM), lambda h, i: (0, OFF_V // V_DIM + h)),
        ],
        out_specs=pl.BlockSpec((tq, V_DIM), lambda h, i: (i, h)),
        out_shape=jax.ShapeDtypeStruct((s, ATTN_WIDTH), BF16),
        scratch_shapes=[
            pltpu.VMEM((2, 1, tq), F32),
            pltpu.VMEM((2, 1, tq), F32),
            pltpu.VMEM((2, V_DIM, tq), F32),
        ],
        compiler_params=pltpu.CompilerParams(
            dimension_semantics=("parallel", "arbitrary"),
            vmem_limit_bytes=VMEM_LIMIT_BYTES),
        name="diff_attn",
    )(lam_params, subln_g, z, z, z)


def _merge_kernel(x_ref, ya_ref, cb_ref, cc_ref, ch_ref, pu_ref, ga_ref, gb_ref, gc_ref,
                  cch_ref, chh_ref, puh_ref,
                  bg_ref, cw_ref, pw_ref, ps_ref, pa_ref, pb_ref, pc_ref, wo_ref, gp_ref,
                  o_ref, ext_ref, *, tm):
    i = pl.program_id(0)
    not_first = (i > 0).astype(F32)

    u = cc_ref[...].astype(F32) * ch_ref[...].astype(F32)
    ext_ref[pl.ds(0, HALO), :] = cch_ref[...].astype(F32) * chh_ref[...].astype(F32) * not_first
    ext_ref[pl.ds(HALO, tm), :] = u
    conv = u * cw_ref[pl.ds(CONV_K - 1, 1), :]
    for j in range(CONV_K - 1):
        back = CONV_K - 1 - j
        conv = conv + ext_ref[pl.ds(HALO - back, tm), :] * cw_ref[pl.ds(j, 1), :]
    y_conv = (cb_ref[...].astype(F32) * conv).astype(BF16)

    pu = pu_ref[...].astype(F32)
    ext_ref[pl.ds(0, HALO), :] = puh_ref[...].astype(F32) * not_first
    ext_ref[pl.ds(HALO, tm), :] = pu
    t1 = (i * tm + 1 + lax.broadcasted_iota(jnp.int32, (tm, 1), 0)).astype(F32)
    groups = []
    for gi, w in enumerate(POOL_WINDOWS):
        cols = pl.ds(gi * POOL_GROUP, POOL_GROUP)
        ug = pu[:, gi * POOL_GROUP:(gi + 1) * POOL_GROUP]
        win = ug
        for back in range(1, w):
            win = win + ext_ref[pl.ds(HALO - back, tm), cols]
        dgrp = win / jnp.minimum(t1, float(w)) - ug
        groups.append(jnp.dot(dgrp.astype(BF16), pw_ref[gi], preferred_element_type=F32))
    y_pool = (jnp.concatenate(groups, axis=-1) * ps_ref[...]).astype(BF16)

    bg = bg_ref[...]
    d = x_ref.shape[-1]
    merged = (jax.nn.sigmoid(ga_ref[...].astype(F32) + bg[:, 0:d])
              * jnp.dot(ya_ref[...], pa_ref[...], preferred_element_type=F32))
    merged += (jax.nn.sigmoid(gb_ref[...].astype(F32) + bg[:, d:2 * d])
               * jnp.dot(y_conv, pb_ref[...], preferred_element_type=F32))
    merged += (jax.nn.sigmoid(gc_ref[...].astype(F32) + bg[:, 2 * d:3 * d])
               * jnp.dot(y_pool, pc_ref[...], preferred_element_type=F32))
    mix = jnp.dot(merged.astype(BF16), wo_ref[...], preferred_element_type=F32)
    o_ref[...] = x_ref[...] + _rms(mix, gp_ref[...])


def _merge(x, z, ya, bg, cw, pw, ps, pa, pb, pc, wo, gp, *, tm):
    s, d = x.shape
    hb = tm // HALO

    def col(width, off):
        return pl.BlockSpec((tm, width), lambda i: (i, off // width))

    def halo(off):
        return pl.BlockSpec((HALO, CONV_WIDTH),
                            lambda i: (jnp.maximum(i * hb - 1, 0), off // CONV_WIDTH))

    def whole(a):
        return pl.BlockSpec(a.shape, lambda i: (0,) * a.ndim)

    kern = functools.partial(_merge_kernel, tm=tm)
    return pl.pallas_call(
        kern,
        grid=(s // tm,),
        in_specs=[
            pl.BlockSpec((tm, d), lambda i: (i, 0)),
            pl.BlockSpec((tm, ATTN_WIDTH), lambda i: (i, 0)),
            col(CONV_WIDTH, OFF_CB), col(CONV_WIDTH, OFF_CC), col(CONV_WIDTH, OFF_CH),
            col(POOL_WIDTH, OFF_POOL),
            col(d, OFF_GATE), col(d, OFF_GATE + d), col(d, OFF_GATE + 2 * d),
            halo(OFF_CC), halo(OFF_CH), halo(OFF_POOL),
            whole(bg), whole(cw), whole(pw), whole(ps), whole(pa), whole(pb), whole(pc),
            whole(wo), whole(gp),
        ],
        out_specs=pl.BlockSpec((tm, d), lambda i: (i, 0)),
        out_shape=jax.ShapeDtypeStruct((s, d), F32),
        scratch_shapes=[pltpu.VMEM((tm + HALO, CONV_WIDTH), F32)],
        compiler_params=pltpu.CompilerParams(
            dimension_semantics=("parallel",),
            vmem_limit_bytes=VMEM_LIMIT_BYTES),
        name="merge",
    )(x, ya, z, z, z, z, z, z, z, z, z, z, bg, cw, pw, ps, pa, pb, pc, wo, gp)


def _ffn_kernel(x_ref, gpre_ref, wg_ref, wu_ref, wd_ref, gpost_ref, o_ref, hn_ref, acc_ref):
    c = pl.program_id(1)

    @pl.when(c == 0)
    def _():
        hn_ref[...] = _rms(x_ref[...], gpre_ref[...]).astype(BF16)
        acc_ref[...] = jnp.zeros(acc_ref.shape, F32)

    hn = hn_ref[...]
    gate = jnp.dot(hn, wg_ref[...], preferred_element_type=F32)
    up = jnp.dot(hn, wu_ref[...], preferred_element_type=F32)
    hf = (gate * jax.nn.sigmoid(gate) * up).astype(BF16)
    acc_ref[...] += jnp.dot(hf, wd_ref[...], preferred_element_type=F32)

    @pl.when(c == pl.num_programs(1) - 1)
    def _():
        o_ref[...] = x_ref[...] + _rms(acc_ref[...], gpost_ref[...])


def _ffn(x, gpre, wgu, wd, gpost, *, tm, fc):
    s, d = x.shape
    d_ff = wd.shape[0]
    nc = d_ff // fc
    return pl.pallas_call(
        _ffn_kernel,
        grid=(s // tm, nc),
        in_specs=[
            pl.BlockSpec((tm, d), lambda i, c: (i, 0)),
            pl.BlockSpec((1, d), lambda i, c: (0, 0)),
            pl.BlockSpec((d, fc), lambda i, c: (0, c)),
            pl.BlockSpec((d, fc), lambda i, c: (0, nc + c)),
            pl.BlockSpec((fc, d), lambda i, c: (c, 0)),
            pl.BlockSpec((1, d), lambda i, c: (0, 0)),
        ],
        out_specs=pl.BlockSpec((tm, d), lambda i, c: (i, 0)),
        out_shape=jax.ShapeDtypeStruct((s, d), F32),
        scratch_shapes=[pltpu.VMEM((tm, d), BF16), pltpu.VMEM((tm, d), F32)],
        compiler_params=pltpu.CompilerParams(
            dimension_semantics=("parallel", "arbitrary"),
            vmem_limit_bytes=VMEM_LIMIT_BYTES),
        name="ffn",
    )(x, gpre, wgu, wgu, wd, gpost)


def _tiles(s):
    pick = lambda pref: min(pref, s)
    return dict(proj_tm=pick(1024), proj_tn=2048, tq=pick(1024), merge_tm=pick(512), ffn_tm=pick(1024))


def kernel(x, g_pre_mix, w_in, b_gate, lambda_q1, lambda_k1, lambda_q2, lambda_k2, subln_g,
           w_attn_proj, conv_w, w_conv_proj, pool_w, pool_scale, w_pool_proj, w_out,
           g_post_mix, g_pre_ffn, w_gate_up, w_down, g_post_ffn):
    b, s, d = x.shape
    depth = w_in.shape[0]
    d_ff = w_down.shape[1]
    t = _tiles(s)
    row = lambda a: a.reshape(1, -1)
    outs = []
    for bi in range(b):
        h = x[bi]
        for l in range(depth):
            lambda_init = 0.8 - 0.6 * math.exp(-0.3 * l)
            z = _in_proj(h, row(g_pre_mix[l]), w_in[l].astype(BF16), tm=t["proj_tm"], tn=t["proj_tn"])
            lam_params = jnp.stack([lambda_q1[l], lambda_k1[l], lambda_q2[l], lambda_k2[l]]).astype(F32)
            ya = _attention(z, lam_params, subln_g[l].reshape(V_DIM, 1), tq=t["tq"],
                            lambda_init=lambda_init)
            h = _merge(h, z, ya, row(b_gate[l]), conv_w[l], pool_w[l].astype(BF16), row(pool_scale[l]),
                       w_attn_proj[l].astype(BF16), w_conv_proj[l].astype(BF16),
                       w_pool_proj[l].astype(BF16), w_out[l].astype(BF16), row(g_post_mix[l]),
                       tm=t["merge_tm"])
            h = _ffn(h, row(g_pre_ffn[l]), w_gate_up[l].astype(BF16), w_down[l].astype(BF16),
                     row(g_post_ffn[l]), tm=t["ffn_tm"], fc=d_ff // 2)
        outs.append(h)
    return jnp.stack(outs)
```

```python
import functools
import math

import jax
import jax.numpy as jnp
from jax import lax
from jax.experimental import pallas as pl
from jax.experimental.pallas import tpu as pltpu

F32 = jnp.float32
BF16 = jnp.bfloat16

EPS = 1e-6
N_HEADS = 8
HEAD_DIM = 64
V_DIM = 2 * HEAD_DIM
ATTN_WIDTH = N_HEADS * V_DIM
CONV_WIDTH = 512
CONV_K = 3
POOL_WINDOWS = (2, 4, 8, 16)
POOL_GROUP = 128
POOL_WIDTH = POOL_GROUP * len(POOL_WINDOWS)
HALO = 16
QCHUNK = 256
VT_CHUNK = 512
LOG2_E = math.log2(math.e)

OFF_K = ATTN_WIDTH
OFF_V = 2 * ATTN_WIDTH
OFF_CB = 3 * ATTN_WIDTH
OFF_CC = OFF_CB + CONV_WIDTH
OFF_CH = OFF_CC + CONV_WIDTH
OFF_POOL = OFF_CH + CONV_WIDTH
OFF_GATE = OFF_POOL + POOL_WIDTH

VMEM_LIMIT_BYTES = 56 * 1024 * 1024


def _rms(x, g):
    return x * lax.rsqrt(jnp.mean(x * x, axis=-1, keepdims=True) + EPS) * g


def _inproj_kernel(x_ref, g_ref, w_ref, z_ref, xn_ref):
    @pl.when(pl.program_id(1) == 0)
    def _():
        xn_ref[...] = _rms(x_ref[...], g_ref[...]).astype(BF16)

    z_ref[...] = jnp.dot(xn_ref[...], w_ref[...], preferred_element_type=F32).astype(BF16)


def _in_proj(x, g, w, *, tm, tn):
    s, d = x.shape
    n = w.shape[1]
    return pl.pallas_call(
        _inproj_kernel,
        grid=(s // tm, n // tn),
        in_specs=[
            pl.BlockSpec((tm, d), lambda i, j: (i, 0)),
            pl.BlockSpec((1, d), lambda i, j: (0, 0)),
            pl.BlockSpec((d, tn), lambda i, j: (0, j)),
        ],
        out_specs=pl.BlockSpec((tm, tn), lambda i, j: (i, j)),
        out_shape=jax.ShapeDtypeStruct((s, n), BF16),
        scratch_shapes=[pltpu.VMEM((tm, d), BF16)],
        compiler_params=pltpu.CompilerParams(
            dimension_semantics=("parallel", "arbitrary"),
            vmem_limit_bytes=VMEM_LIMIT_BYTES),
        name="in_proj",
    )(x, g, w)


def _attn_kernel(lam_ref, g_ref, q_ref, k_ref, v_ref, o_ref,
                 vt_ref, qh_ref, m_ref, l_ref, acc_ref, *, tq, lambda_init):
    qi = pl.program_id(1)
    tk = tq
    s_len = k_ref.shape[0]

    @pl.when(qi == 0)
    def _():
        for r in range(0, s_len, VT_CHUNK):
            vt_ref[:, pl.ds(r, VT_CHUNK)] = v_ref[pl.ds(r, VT_CHUNK), :].T

    q = (q_ref[...].astype(F32) * (HEAD_DIM ** -0.5 * LOG2_E)).astype(BF16)
    lane = lax.broadcasted_iota(jnp.int32, (tq, V_DIM), 1)
    zero = jnp.zeros_like(q)
    qh_ref[0] = jnp.where(lane < HEAD_DIM, q, zero)
    qh_ref[1] = jnp.where(lane >= HEAD_DIM, q, zero)

    units = [(c, u) for u in range(tq // QCHUNK) for c in range(2)]

    def block(j, *, diagonal, online):
        start = pl.multiple_of(j * tk, tk)

        def n_keys(u):
            return (u + 1) * QCHUNK if diagonal else tk

        def scores(c, u):
            kb = k_ref[pl.ds(start, n_keys(u)), :]
            qc = qh_ref[c, pl.ds(u * QCHUNK, QCHUNK), :]
            return lax.dot_general(kb, qc, (((1,), (1,)), ((), ())),
                                   preferred_element_type=F32)

        st_next = scores(*units[0])
        for idx, (c, u) in enumerate(units):
            st = st_next
            if idx + 1 < len(units):
                st_next = scores(*units[idx + 1])
            nk = n_keys(u)
            cols = pl.ds(u * QCHUNK, QCHUNK)
            if diagonal:
                krow = lax.broadcasted_iota(jnp.int32, (nk, QCHUNK), 0)
                qcol = lax.broadcasted_iota(jnp.int32, (nk, QCHUNK), 1) + u * QCHUNK
                st = jnp.where(krow <= qcol, st, -jnp.inf)
            vt = vt_ref[:, pl.ds(start, nk)]
            if online:
                m_old = m_ref[c, :, cols]
                m_new = jnp.maximum(m_old, jnp.max(st, axis=0, keepdims=True))
                p = jnp.exp2(st - m_new)
                alpha = jnp.exp2(m_old - m_new)
                l_ref[c, :, cols] = alpha * l_ref[c, :, cols] + jnp.sum(p, axis=0, keepdims=True)
                m_ref[c, :, cols] = m_new
                pv = jnp.dot(vt, p.astype(BF16), preferred_element_type=F32)
                acc_ref[c, :, cols] = alpha * acc_ref[c, :, cols] + pv
            else:
                p = jnp.exp2(st - m_ref[c, :, cols])
                l_ref[c, :, cols] += jnp.sum(p, axis=0, keepdims=True)
                acc_ref[c, :, cols] += jnp.dot(vt, p.astype(BF16), preferred_element_type=F32)

    def all_blocks(online):
        m_ref[...] = jnp.full(m_ref.shape, -jnp.inf, F32)
        l_ref[...] = jnp.zeros(l_ref.shape, F32)
        acc_ref[...] = jnp.zeros(acc_ref.shape, F32)
        block(qi, diagonal=True, online=True)

        def body(j, carry):
            block(j, diagonal=False, online=online)
            return carry

        lax.fori_loop(0, qi, body, 0)

    all_blocks(online=False)
    overflow_probe = jnp.sum(acc_ref[...] * 0.0) + jnp.sum(l_ref[...] * 0.0)

    @pl.when(jnp.logical_not(overflow_probe == 0.0))
    def _():
        all_blocks(online=True)

    lq1, lk1, lq2, lk2 = (lam_ref[pl.ds(r, 1), :] for r in range(4))
    lam = (jnp.exp(jnp.sum(lq1 * lk1, axis=-1, keepdims=True))
           - jnp.exp(jnp.sum(lq2 * lk2, axis=-1, keepdims=True)) + lambda_init)
    o = acc_ref[0] / l_ref[0] - lam * (acc_ref[1] / l_ref[1])
    ms = jnp.mean(o * o, axis=0, keepdims=True)
    on = o * lax.rsqrt(ms + EPS) * g_ref[...] * (1.0 - lambda_init)
    o_ref[...] = on.T.astype(BF16)


def _attention(z, lam_params, subln_g, *, tq, lambda_init):
    s = z.shape[0]
    kern = functools.partial(_attn_kernel, tq=tq, lambda_init=lambda_init)
    return pl.pallas_call(
        kern,
        grid=(N_HEADS, s // tq),
        in_specs=[
            pl.BlockSpec((4, HEAD_DIM), lambda h, i: (0, 0)),
            pl.BlockSpec((V_DIM, 1), lambda h, i: (0, 0)),
            pl.BlockSpec((tq, V_DIM), lambda h, i: (i, h)),
            pl.BlockSpec((s, V_DIM), lambda h, i: (0, OFF_K // V_DIM + h)),
            pl.BlockSpec((s, V_DIM), lambda h, i: (0, OFF_V // V_DIM + h)),
        ],
        out_specs=pl.BlockSpec((tq, V_DIM), lambda h, i: (i, h)),
        out_shape=jax.ShapeDtypeStruct((s, ATTN_WIDTH), BF16),
        scratch_shapes=[
            pltpu.VMEM((V_DIM, s), BF16),
            pltpu.VMEM((2, tq, V_DIM), BF16),
            pltpu.VMEM((2, 1, tq), F32),
            pltpu.VMEM((2, 1, tq), F32),
            pltpu.VMEM((2, V_DIM, tq), F32),
        ],
        compiler_params=pltpu.CompilerParams(
            dimension_semantics=("parallel", "arbitrary"),
            vmem_limit_bytes=VMEM_LIMIT_BYTES),
        name="diff_attn",
    )(lam_params, subln_g, z, z, z)


def _merge_kernel(x_ref, ya_ref, cb_ref, cc_ref, ch_ref, pu_ref, ga_ref, gb_ref, gc_ref,
                  cch_ref, chh_ref, puh_ref,
                  bg_ref, cw_ref, pw_ref, ps_ref, pa_ref, pb_ref, pc_ref, wo_ref, gp_ref,
                  o_ref, ext_ref, *, tm):
    i = pl.program_id(0)
    not_first = (i > 0).astype(F32)

    u = cc_ref[...].astype(F32) * ch_ref[...].astype(F32)
    ext_ref[pl.ds(0, HALO), :] = cch_ref[...].astype(F32) * chh_ref[...].astype(F32) * not_first
    ext_ref[pl.ds(HALO, tm), :] = u
    conv = u * cw_ref[pl.ds(CONV_K - 1, 1), :]
    for j in range(CONV_K - 1):
        back = CONV_K - 1 - j
        conv = conv + ext_ref[pl.ds(HALO - back, tm), :] * cw_ref[pl.ds(j, 1), :]
    y_conv = (cb_ref[...].astype(F32) * conv).astype(BF16)

    pu = pu_ref[...].astype(F32)
    ext_ref[pl.ds(0, HALO), :] = puh_ref[...].astype(F32) * not_first
    ext_ref[pl.ds(HALO, tm), :] = pu
    t1 = (i * tm + 1 + lax.broadcasted_iota(jnp.int32, (tm, 1), 0)).astype(F32)
    groups = []
    for gi, w in enumerate(POOL_WINDOWS):
        cols = pl.ds(gi * POOL_GROUP, POOL_GROUP)
        ug = pu[:, gi * POOL_GROUP:(gi + 1) * POOL_GROUP]
        win = ug
        for back in range(1, w):
            win = win + ext_ref[pl.ds(HALO - back, tm), cols]
        dgrp = win / jnp.minimum(t1, float(w)) - ug
        groups.append(jnp.dot(dgrp.astype(BF16), pw_ref[gi], preferred_element_type=F32))
    y_pool = (jnp.concatenate(groups, axis=-1) * ps_ref[...]).astype(BF16)

    bg = bg_ref[...]
    d = x_ref.shape[-1]
    merged = (jax.nn.sigmoid(ga_ref[...].astype(F32) + bg[:, 0:d])
              * jnp.dot(ya_ref[...], pa_ref[...], preferred_element_type=F32))
    merged += (jax.nn.sigmoid(gb_ref[...].astype(F32) + bg[:, d:2 * d])
               * jnp.dot(y_conv, pb_ref[...], preferred_element_type=F32))
    merged += (jax.nn.sigmoid(gc_ref[...].astype(F32) + bg[:, 2 * d:3 * d])
               * jnp.dot(y_pool, pc_ref[...], preferred_element_type=F32))
    mix = jnp.dot(merged.astype(BF16), wo_ref[...], preferred_element_type=F32)
    o_ref[...] = x_ref[...] + _rms(mix, gp_ref[...])


def _merge(x, z, ya, bg, cw, pw, ps, pa, pb, pc, wo, gp, *, tm):
    s, d = x.shape
    hb = tm // HALO

    def col(width, off):
        return pl.BlockSpec((tm, width), lambda i: (i, off // width))

    def halo(off):
        return pl.BlockSpec((HALO, CONV_WIDTH),
                            lambda i: (jnp.maximum(i * hb - 1, 0), off // CONV_WIDTH))

    def whole(a):
        return pl.BlockSpec(a.shape, lambda i: (0,) * a.ndim)

    kern = functools.partial(_merge_kernel, tm=tm)
    return pl.pallas_call(
        kern,
        grid=(s // tm,),
        in_specs=[
            pl.BlockSpec((tm, d), lambda i: (i, 0)),
            pl.BlockSpec((tm, ATTN_WIDTH), lambda i: (i, 0)),
            col(CONV_WIDTH, OFF_CB), col(CONV_WIDTH, OFF_CC), col(CONV_WIDTH, OFF_CH),
            col(POOL_WIDTH, OFF_POOL),
            col(d, OFF_GATE), col(d, OFF_GATE + d), col(d, OFF_GATE + 2 * d),
            halo(OFF_CC), halo(OFF_CH), halo(OFF_POOL),
            whole(bg), whole(cw), whole(pw), whole(ps), whole(pa), whole(pb), whole(pc),
            whole(wo), whole(gp),
        ],
        out_specs=pl.BlockSpec((tm, d), lambda i: (i, 0)),
        out_shape=jax.ShapeDtypeStruct((s, d), F32),
        scratch_shapes=[pltpu.VMEM((tm + HALO, CONV_WIDTH), F32)],
        compiler_params=pltpu.CompilerParams(
            dimension_semantics=("parallel",),
            vmem_limit_bytes=VMEM_LIMIT_BYTES),
        name="merge",
    )(x, ya, z, z, z, z, z, z, z, z, z, z, bg, cw, pw, ps, pa, pb, pc, wo, gp)


def _ffn_kernel(x_ref, gpre_ref, wg_ref, wu_ref, wd_ref, gpost_ref, o_ref, hn_ref, acc_ref):
    c = pl.program_id(1)

    @pl.when(c == 0)
    def _():
        hn_ref[...] = _rms(x_ref[...], gpre_ref[...]).astype(BF16)
        acc_ref[...] = jnp.zeros(acc_ref.shape, F32)

    hn = hn_ref[...]
    gate = jnp.dot(hn, wg_ref[...], preferred_element_type=F32)
    up = jnp.dot(hn, wu_ref[...], preferred_element_type=F32)
    hf = (gate * jax.nn.sigmoid(gate) * up).astype(BF16)
    acc_ref[...] += jnp.dot(hf, wd_ref[...], preferred_element_type=F32)

    @pl.when(c == pl.num_programs(1) - 1)
    def _():
        o_ref[...] = x_ref[...] + _rms(acc_ref[...], gpost_ref[...])


def _ffn(x, gpre, wgu, wd, gpost, *, tm, fc):
    s, d = x.shape
    d_ff = wd.shape[0]
    nc = d_ff // fc
    return pl.pallas_call(
        _ffn_kernel,
        grid=(s // tm, nc),
        in_specs=[
            pl.BlockSpec((tm, d), lambda i, c: (i, 0)),
            pl.BlockSpec((1, d), lambda i, c: (0, 0)),
            pl.BlockSpec((d, fc), lambda i, c: (0, c)),
            pl.BlockSpec((d, fc), lambda i, c: (0, nc + c)),
            pl.BlockSpec((fc, d), lambda i, c: (c, 0)),
            pl.BlockSpec((1, d), lambda i, c: (0, 0)),
        ],
        out_specs=pl.BlockSpec((tm, d), lambda i, c: (i, 0)),
        out_shape=jax.ShapeDtypeStruct((s, d), F32),
        scratch_shapes=[pltpu.VMEM((tm, d), BF16), pltpu.VMEM((tm, d), F32)],
        compiler_params=pltpu.CompilerParams(
            dimension_semantics=("parallel", "arbitrary"),
            vmem_limit_bytes=VMEM_LIMIT_BYTES),
        name="ffn",
    )(x, gpre, wgu, wgu, wd, gpost)


def _tiles(s):
    pick = lambda pref: min(pref, s)
    return dict(proj_tm=pick(1024), proj_tn=2048, tq=pick(1024), merge_tm=pick(512), ffn_tm=pick(1024))


def kernel(x, g_pre_mix, w_in, b_gate, lambda_q1, lambda_k1, lambda_q2, lambda_k2, subln_g,
           w_attn_proj, conv_w, w_conv_proj, pool_w, pool_scale, w_pool_proj, w_out,
           g_post_mix, g_pre_ffn, w_gate_up, w_down, g_post_ffn):
    b, s, d = x.shape
    depth = w_in.shape[0]
    d_ff = w_down.shape[1]
    t = _tiles(s)
    row = lambda a: a.reshape(1, -1)
    outs = []
    for bi in range(b):
        h = x[bi]
        for l in range(depth):
            lambda_init = 0.8 - 0.6 * math.exp(-0.3 * l)
            z = _in_proj(h, row(g_pre_mix[l]), w_in[l].astype(BF16), tm=t["proj_tm"], tn=t["proj_tn"])
            lam_params = jnp.stack([lambda_q1[l], lambda_k1[l], lambda_q2[l], lambda_k2[l]]).astype(F32)
            ya = _attention(z, lam_params, subln_g[l].reshape(V_DIM, 1), tq=t["tq"],
                            lambda_init=lambda_init)
            h = _merge(h, z, ya, row(b_gate[l]), conv_w[l], pool_w[l].astype(BF16), row(pool_scale[l]),
                       w_attn_proj[l].astype(BF16), w_conv_proj[l].astype(BF16),
                       w_pool_proj[l].astype(BF16), w_out[l].astype(BF16), row(g_post_mix[l]),
                       tm=t["merge_tm"])
            h = _ffn(h, row(g_pre_ffn[l]), w_gate_up[l].astype(BF16), w_down[l].astype(BF16),
                     row(g_post_ffn[l]), tm=t["ffn_tm"], fc=d_ff // 2)
        outs.append(h)
    return jnp.stack(outs)
```

```python
import functools
import math

import jax
import jax.numpy as jnp
from jax import lax
from jax.experimental import pallas as pl
from jax.experimental.pallas import tpu as pltpu

F32 = jnp.float32
BF16 = jnp.bfloat16
LANES = 128

EPS = 1e-6
N_HEADS = 8
HEAD_DIM = 64
V_DIM = 2 * HEAD_DIM
ATTN_WIDTH = N_HEADS * V_DIM
CONV_WIDTH = 512
CONV_K = 3
POOL_WINDOWS = (2, 4, 8, 16)
POOL_GROUP = 128
POOL_WIDTH = POOL_GROUP * len(POOL_WINDOWS)
HALO = 16
QCHUNK = 256
VT_CHUNK = 512
LOG2_E = math.log2(math.e)

OFF_K = ATTN_WIDTH
OFF_V = 2 * ATTN_WIDTH
OFF_CB = 3 * ATTN_WIDTH
OFF_CC = OFF_CB + CONV_WIDTH
OFF_CH = OFF_CC + CONV_WIDTH
OFF_POOL = OFF_CH + CONV_WIDTH
OFF_GATE = OFF_POOL + POOL_WIDTH

VMEM_LIMIT_BYTES = 56 * 1024 * 1024


def _rms(x, g):
    return x * lax.rsqrt(jnp.mean(x * x, axis=-1, keepdims=True) + EPS) * g


def _inproj_kernel(x_ref, g_ref, w_ref, z_ref, xn_ref):
    @pl.when(pl.program_id(1) == 0)
    def _():
        xn_ref[...] = _rms(x_ref[...], g_ref[...]).astype(BF16)

    z = jnp.dot(xn_ref[...], w_ref[...], preferred_element_type=F32).astype(BF16)
    for c in range(z_ref.shape[0]):
        z_ref[c] = z[:, c * LANES:(c + 1) * LANES]


def _in_proj(x, g, w, *, tm, tn):
    s, d = x.shape
    n = w.shape[1]
    return pl.pallas_call(
        _inproj_kernel,
        grid=(s // tm, n // tn),
        in_specs=[
            pl.BlockSpec((tm, d), lambda i, j: (i, 0)),
            pl.BlockSpec((1, d), lambda i, j: (0, 0)),
            pl.BlockSpec((d, tn), lambda i, j: (0, j)),
        ],
        out_specs=pl.BlockSpec((tn // LANES, tm, LANES), lambda i, j: (j, i, 0)),
        out_shape=jax.ShapeDtypeStruct((n // LANES, s, LANES), BF16),
        scratch_shapes=[pltpu.VMEM((tm, d), BF16)],
        compiler_params=pltpu.CompilerParams(
            dimension_semantics=("parallel", "arbitrary"),
            vmem_limit_bytes=VMEM_LIMIT_BYTES),
        name="in_proj",
    )(x, g, w)


def _attn_kernel(lam_ref, g_ref, q_ref, k_ref, v_ref, o_ref,
                 vt_ref, qh_ref, st_ref, m_ref, l_ref, acc_ref, *, tq, lambda_init):
    qi = pl.program_id(1)
    tk = tq
    s_len = k_ref.shape[0]

    @pl.when(qi == 0)
    def _():
        for r in range(0, s_len, VT_CHUNK):
            vt_ref[:, pl.ds(r, VT_CHUNK)] = v_ref[pl.ds(r, VT_CHUNK), :].T

    q = (q_ref[...].astype(F32) * (HEAD_DIM ** -0.5 * LOG2_E)).astype(BF16)
    lane = lax.broadcasted_iota(jnp.int32, (tq, V_DIM), 1)
    zero = jnp.zeros_like(q)
    qh_ref[0] = jnp.where(lane < HEAD_DIM, q, zero)
    qh_ref[1] = jnp.where(lane >= HEAD_DIM, q, zero)

    units = [(c, u) for u in range(tq // QCHUNK) for c in range(2)]

    def scores(start, nk, c, u):
        kb = k_ref[pl.ds(start, nk), :]
        qc = qh_ref[c, pl.ds(u * QCHUNK, QCHUNK), :]
        return lax.dot_general(kb, qc, (((1,), (1,)), ((), ())),
                               preferred_element_type=F32)

    def block(j, *, diagonal, mode, carried=False):
        start = pl.multiple_of(j * tk, tk)

        def n_keys(u):
            return (u + 1) * QCHUNK if diagonal else tk

        if diagonal:
            tri = (lax.broadcasted_iota(jnp.int32, (QCHUNK, QCHUNK), 0)
                   <= lax.broadcasted_iota(jnp.int32, (QCHUNK, QCHUNK), 1))

        st_next = st_ref[...] if carried else scores(start, n_keys(units[0][1]), *units[0])
        for idx, (c, u) in enumerate(units):
            st = st_next
            if idx + 1 < len(units):
                c_next, u_next = units[idx + 1]
                st_next = scores(start, n_keys(u_next), c_next, u_next)
            elif carried:
                st_ref[...] = scores(pl.multiple_of((j + 1) * tk, tk), tk, *units[0])
            nk = n_keys(u)
            cols = pl.ds(u * QCHUNK, QCHUNK)
            if diagonal:
                last = jnp.where(tri, st[nk - QCHUNK:], -jnp.inf)
                st = last if nk == QCHUNK else jnp.concatenate([st[:nk - QCHUNK], last], axis=0)
            vt = vt_ref[:, pl.ds(start, nk)]
            if mode == "online":
                m_old = m_ref[c, :, cols]
                m_new = jnp.maximum(m_old, jnp.max(st, axis=0, keepdims=True))
                p = jnp.exp2(st - m_new)
                alpha = jnp.exp2(m_old - m_new)
                l_ref[c, :, cols] = alpha * l_ref[c, :, cols] + jnp.sum(p, axis=0, keepdims=True)
                m_ref[c, :, cols] = m_new
                pv = jnp.dot(vt, p.astype(BF16), preferred_element_type=F32)
                acc_ref[c, :, cols] = alpha * acc_ref[c, :, cols] + pv
            elif mode == "fixed":
                p = jnp.exp2(st - m_ref[c, :, cols])
                l_ref[c, :, cols] += jnp.sum(p, axis=0, keepdims=True)
                acc_ref[c, :, cols] += jnp.dot(vt, p.astype(BF16), preferred_element_type=F32)
            else:
                m = jnp.max(last, axis=0, keepdims=True)
                p = jnp.exp2(st - m)
                m_ref[c, :, cols] = m
                l_ref[c, :, cols] = jnp.sum(p, axis=0, keepdims=True)
                acc_ref[c, :, cols] = jnp.dot(vt, p.astype(BF16), preferred_element_type=F32)

    def all_blocks(online):
        if online:
            m_ref[...] = jnp.full(m_ref.shape, -jnp.inf, F32)
            l_ref[...] = jnp.zeros(l_ref.shape, F32)
            acc_ref[...] = jnp.zeros(acc_ref.shape, F32)
        block(qi, diagonal=True, mode="online" if online else "start")
        st_ref[...] = scores(0, tk, *units[0])

        def body(j, carry):
            block(j, diagonal=False, mode="online" if online else "fixed", carried=True)
            return carry

        lax.fori_loop(0, qi, body, 0)

    all_blocks(online=False)
    overflow_probe = jnp.sum(acc_ref[...] * 0.0) + jnp.sum(l_ref[...] * 0.0)

    @pl.when(jnp.logical_not(overflow_probe == 0.0))
    def _():
        all_blocks(online=True)

    lq1, lk1, lq2, lk2 = (lam_ref[pl.ds(r, 1), :] for r in range(4))
    lam = (jnp.exp(jnp.sum(lq1 * lk1, axis=-1, keepdims=True))
           - jnp.exp(jnp.sum(lq2 * lk2, axis=-1, keepdims=True)) + lambda_init)
    o = acc_ref[0] / l_ref[0] - lam * (acc_ref[1] / l_ref[1])
    ms = jnp.mean(o * o, axis=0, keepdims=True)
    on = o * lax.rsqrt(ms + EPS) * g_ref[...] * (1.0 - lambda_init)
    o_ref[...] = on.T.astype(BF16)


def _attention(z, lam_params, subln_g, *, tq, lambda_init):
    s = z.shape[1]
    kern = functools.partial(_attn_kernel, tq=tq, lambda_init=lambda_init)
    return pl.pallas_call(
        kern,
        grid=(N_HEADS, s // tq),
        in_specs=[
            pl.BlockSpec((4, HEAD_DIM), lambda h, i: (0, 0)),
            pl.BlockSpec((V_DIM, 1), lambda h, i: (0, 0)),
            pl.BlockSpec((None, tq, V_DIM), lambda h, i: (h, i, 0)),
            pl.BlockSpec((None, s, V_DIM), lambda h, i: (OFF_K // V_DIM + h, 0, 0)),
            pl.BlockSpec((None, s, V_DIM), lambda h, i: (OFF_V // V_DIM + h, 0, 0)),
        ],
        out_specs=pl.BlockSpec((tq, V_DIM), lambda h, i: (i, h)),
        out_shape=jax.ShapeDtypeStruct((s, ATTN_WIDTH), BF16),
        scratch_shapes=[
            pltpu.VMEM((V_DIM, s), BF16),
            pltpu.VMEM((2, tq, V_DIM), BF16),
            pltpu.VMEM((tq, QCHUNK), F32),
            pltpu.VMEM((2, 1, tq), F32),
            pltpu.VMEM((2, 1, tq), F32),
            pltpu.VMEM((2, V_DIM, tq), F32),
        ],
        compiler_params=pltpu.CompilerParams(
            dimension_semantics=("parallel", "arbitrary"),
            vmem_limit_bytes=VMEM_LIMIT_BYTES),
        name="diff_attn",
    )(lam_params, subln_g, z, z, z)


def _merge_kernel(x_ref, ya_ref, cb_ref, cc_ref, ch_ref, pu_ref, ga_ref, gb_ref, gc_ref,
                  cch_ref, chh_ref, puh_ref,
                  bg_ref, cw_ref, pw_ref, ps_ref, pa_ref, pb_ref, pc_ref, wo_ref, gp_ref,
                  o_ref, ext_ref, *, tm):
    i = pl.program_id(0)
    not_first = (i > 0).astype(F32)

    def wide(ref):
        return jnp.concatenate([ref[c] for c in range(ref.shape[0])], axis=-1).astype(F32)

    u = wide(cc_ref) * wide(ch_ref)
    ext_ref[pl.ds(0, HALO), :] = wide(cch_ref) * wide(chh_ref) * not_first
    ext_ref[pl.ds(HALO, tm), :] = u
    conv = u * cw_ref[pl.ds(CONV_K - 1, 1), :]
    for j in range(CONV_K - 1):
        back = CONV_K - 1 - j
        conv = conv + ext_ref[pl.ds(HALO - back, tm), :] * cw_ref[pl.ds(j, 1), :]
    y_conv = (wide(cb_ref) * conv).astype(BF16)

    pu = wide(pu_ref)
    ext_ref[pl.ds(0, HALO), :] = wide(puh_ref) * not_first
    ext_ref[pl.ds(HALO, tm), :] = pu
    t1 = (i * tm + 1 + lax.broadcasted_iota(jnp.int32, (tm, 1), 0)).astype(F32)
    groups = []
    for gi, w in enumerate(POOL_WINDOWS):
        cols = pl.ds(gi * POOL_GROUP, POOL_GROUP)
        ug = pu[:, gi * POOL_GROUP:(gi + 1) * POOL_GROUP]
        win = ug
        for back in range(1, w):
            win = win + ext_ref[pl.ds(HALO - back, tm), cols]
        dgrp = win / jnp.minimum(t1, float(w)) - ug
        groups.append(jnp.dot(dgrp.astype(BF16), pw_ref[gi], preferred_element_type=F32))
    y_pool = (jnp.concatenate(groups, axis=-1) * ps_ref[...]).astype(BF16)

    bg = bg_ref[...]
    d = x_ref.shape[-1]
    merged = (jax.nn.sigmoid(wide(ga_ref) + bg[:, 0:d])
              * jnp.dot(ya_ref[...], pa_ref[...], preferred_element_type=F32))
    merged += (jax.nn.sigmoid(wide(gb_ref) + bg[:, d:2 * d])
               * jnp.dot(y_conv, pb_ref[...], preferred_element_type=F32))
    merged += (jax.nn.sigmoid(wide(gc_ref) + bg[:, 2 * d:3 * d])
               * jnp.dot(y_pool, pc_ref[...], preferred_element_type=F32))
    mix = jnp.dot(merged.astype(BF16), wo_ref[...], preferred_element_type=F32)
    o_ref[...] = x_ref[...] + _rms(mix, gp_ref[...])


def _merge(x, z, ya, bg, cw, pw, ps, pa, pb, pc, wo, gp, *, tm):
    s, d = x.shape
    hb = tm // HALO

    def col(width, off):
        return pl.BlockSpec((width // LANES, tm, LANES), lambda i: (off // width, i, 0))

    def halo(off):
        return pl.BlockSpec((CONV_WIDTH // LANES, HALO, LANES),
                            lambda i: (off // CONV_WIDTH, jnp.maximum(i * hb - 1, 0), 0))

    def whole(a):
        return pl.BlockSpec(a.shape, lambda i: (0,) * a.ndim)

    kern = functools.partial(_merge_kernel, tm=tm)
    return pl.pallas_call(
        kern,
        grid=(s // tm,),
        in_specs=[
            pl.BlockSpec((tm, d), lambda i: (i, 0)),
            pl.BlockSpec((tm, ATTN_WIDTH), lambda i: (i, 0)),
            col(CONV_WIDTH, OFF_CB), col(CONV_WIDTH, OFF_CC), col(CONV_WIDTH, OFF_CH),
            col(POOL_WIDTH, OFF_POOL),
            col(d, OFF_GATE), col(d, OFF_GATE + d), col(d, OFF_GATE + 2 * d),
            halo(OFF_CC), halo(OFF_CH), halo(OFF_POOL),
            whole(bg), whole(cw), whole(pw), whole(ps), whole(pa), whole(pb), whole(pc),
            whole(wo), whole(gp),
        ],
        out_specs=pl.BlockSpec((tm, d), lambda i: (i, 0)),
        out_shape=jax.ShapeDtypeStruct((s, d), F32),
        scratch_shapes=[pltpu.VMEM((tm + HALO, CONV_WIDTH), F32)],
        compiler_params=pltpu.CompilerParams(
            dimension_semantics=("parallel",),
            vmem_limit_bytes=VMEM_LIMIT_BYTES),
        name="merge",
    )(x, ya, z, z, z, z, z, z, z, z, z, z, bg, cw, pw, ps, pa, pb, pc, wo, gp)


def _ffn_kernel(x_ref, gpre_ref, wg_ref, wu_ref, wd_ref, gpost_ref, o_ref, hn_ref, acc_ref):
    c = pl.program_id(1)

    @pl.when(c == 0)
    def _():
        hn_ref[...] = _rms(x_ref[...], gpre_ref[...]).astype(BF16)
        acc_ref[...] = jnp.zeros(acc_ref.shape, F32)

    hn = hn_ref[...]
    gate = jnp.dot(hn, wg_ref[...], preferred_element_type=F32)
    up = jnp.dot(hn, wu_ref[...], preferred_element_type=F32)
    hf = (gate * jax.nn.sigmoid(gate) * up).astype(BF16)
    acc_ref[...] += jnp.dot(hf, wd_ref[...], preferred_element_type=F32)

    @pl.when(c == pl.num_programs(1) - 1)
    def _():
        o_ref[...] = x_ref[...] + _rms(acc_ref[...], gpost_ref[...])


def _ffn(x, gpre, wgu, wd, gpost, *, tm, fc):
    s, d = x.shape
    d_ff = wd.shape[0]
    nc = d_ff // fc
    return pl.pallas_call(
        _ffn_kernel,
        grid=(s // tm, nc),
        in_specs=[
            pl.BlockSpec((tm, d), lambda i, c: (i, 0)),
            pl.BlockSpec((1, d), lambda i, c: (0, 0)),
            pl.BlockSpec((d, fc), lambda i, c: (0, c)),
            pl.BlockSpec((d, fc), lambda i, c: (0, nc + c)),
            pl.BlockSpec((fc, d), lambda i, c: (c, 0)),
            pl.BlockSpec((1, d), lambda i, c: (0, 0)),
        ],
        out_specs=pl.BlockSpec((tm, d), lambda i, c: (i, 0)),
        out_shape=jax.ShapeDtypeStruct((s, d), F32),
        scratch_shapes=[pltpu.VMEM((tm, d), BF16), pltpu.VMEM((tm, d), F32)],
        compiler_params=pltpu.CompilerParams(
            dimension_semantics=("parallel", "arbitrary"),
            vmem_limit_bytes=VMEM_LIMIT_BYTES),
        name="ffn",
    )(x, gpre, wgu, wgu, wd, gpost)


def _tiles(s):
    pick = lambda pref: min(pref, s)
    return dict(proj_tm=pick(1024), proj_tn=2048, tq=pick(1024), merge_tm=pick(512), ffn_tm=pick(1024))


def kernel(x, g_pre_mix, w_in, b_gate, lambda_q1, lambda_k1, lambda_q2, lambda_k2, subln_g,
           w_attn_proj, conv_w, w_conv_proj, pool_w, pool_scale, w_pool_proj, w_out,
           g_post_mix, g_pre_ffn, w_gate_up, w_down, g_post_ffn):
    b, s, d = x.shape
    depth = w_in.shape[0]
    d_ff = w_down.shape[1]
    t = _tiles(s)
    row = lambda a: a.reshape(1, -1)
    outs = []
    for bi in range(b):
        h = x[bi]
        for l in range(depth):
            lambda_init = 0.8 - 0.6 * math.exp(-0.3 * l)
            z = _in_proj(h, row(g_pre_mix[l]), w_in[l].astype(BF16), tm=t["proj_tm"], tn=t["proj_tn"])
            lam_params = jnp.stack([lambda_q1[l], lambda_k1[l], lambda_q2[l], lambda_k2[l]]).astype(F32)
            ya = _attention(z, lam_params, subln_g[l].reshape(V_DIM, 1), tq=t["tq"],
                            lambda_init=lambda_init)
            h = _merge(h, z, ya, row(b_gate[l]), conv_w[l], pool_w[l].astype(BF16), row(pool_scale[l]),
                       w_attn_proj[l].astype(BF16), w_conv_proj[l].astype(BF16),
                       w_pool_proj[l].astype(BF16), w_out[l].astype(BF16), row(g_post_mix[l]),
                       tm=t["merge_tm"])
            h = _ffn(h, row(g_pre_ffn[l]), w_gate_up[l].astype(BF16), w_down[l].astype(BF16),
                     row(g_post_ffn[l]), tm=t["ffn_tm"], fc=d_ff // 2)
        outs.append(h)
    return jnp.stack(outs)
```

```python
import functools
import math

import jax
import jax.numpy as jnp
from jax import lax
from jax.experimental import pallas as pl
from jax.experimental.pallas import tpu as pltpu

F32 = jnp.float32
BF16 = jnp.bfloat16
LANES = 128

EPS = 1e-6
N_HEADS = 8
HEAD_DIM = 64
V_DIM = 2 * HEAD_DIM
ATTN_WIDTH = N_HEADS * V_DIM
CONV_WIDTH = 512
CONV_K = 3
POOL_WINDOWS = (2, 4, 8, 16)
POOL_GROUP = 128
POOL_WIDTH = POOL_GROUP * len(POOL_WINDOWS)
HALO = 16
QCHUNK = 256
VT_CHUNK = 512
LOG2_E = math.log2(math.e)

OFF_K = ATTN_WIDTH
OFF_V = 2 * ATTN_WIDTH
OFF_CB = 3 * ATTN_WIDTH
OFF_CC = OFF_CB + CONV_WIDTH
OFF_CH = OFF_CC + CONV_WIDTH
OFF_POOL = OFF_CH + CONV_WIDTH
OFF_GATE = OFF_POOL + POOL_WIDTH

VMEM_LIMIT_BYTES = 56 * 1024 * 1024


def _layer_block(a, layer, block=None, index=None):
    rest = a.shape[1:] if block is None else block
    if index is None:
        index = lambda *grid: (0,) * len(rest)
    return pl.BlockSpec((None,) + tuple(rest), lambda *grid: (layer,) + tuple(index(*grid)))


def _rms(x, g):
    return x * lax.rsqrt(jnp.mean(x * x, axis=-1, keepdims=True) + EPS) * g


def _inproj_kernel(x_ref, g_ref, w_ref, z_ref, xn_ref):
    @pl.when(pl.program_id(1) == 0)
    def _():
        xn_ref[...] = _rms(x_ref[...], g_ref[...]).astype(BF16)

    z = jnp.dot(xn_ref[...], w_ref[...], preferred_element_type=F32).astype(BF16)
    for c in range(z_ref.shape[0]):
        z_ref[c] = z[:, c * LANES:(c + 1) * LANES]


def _in_proj(x, g, w, layer, *, tm, tn):
    s, d = x.shape
    n = w.shape[-1]
    return pl.pallas_call(
        _inproj_kernel,
        grid=(s // tm, n // tn),
        in_specs=[
            pl.BlockSpec((tm, d), lambda i, j: (i, 0)),
            _layer_block(g, layer),
            _layer_block(w, layer, (d, tn), lambda i, j: (0, j)),
        ],
        out_specs=pl.BlockSpec((tn // LANES, tm, LANES), lambda i, j: (j, i, 0)),
        out_shape=jax.ShapeDtypeStruct((n // LANES, s, LANES), BF16),
        scratch_shapes=[pltpu.VMEM((tm, d), BF16)],
        compiler_params=pltpu.CompilerParams(
            dimension_semantics=("parallel", "arbitrary"),
            vmem_limit_bytes=VMEM_LIMIT_BYTES),
        name="in_proj",
    )(x, g, w)


def _attn_kernel(lam_ref, g_ref, q_ref, k_ref, v_ref, o_ref,
                 vt_ref, qh_ref, st_ref, m_ref, l_ref, acc_ref, *, tq, tk, lambda_init):
    qi = pl.program_id(1)
    s_len = k_ref.shape[0]

    @pl.when(qi == 0)
    def _():
        for r in range(0, s_len, VT_CHUNK):
            vt_ref[:, pl.ds(r, VT_CHUNK)] = v_ref[pl.ds(r, VT_CHUNK), :].T

    q = (q_ref[...].astype(F32) * (HEAD_DIM ** -0.5 * LOG2_E)).astype(BF16)
    qt = q.T
    feat = lax.broadcasted_iota(jnp.int32, (V_DIM, tq), 0)
    zero = jnp.zeros_like(qt)
    qh_ref[0] = jnp.where(feat < HEAD_DIM, qt, zero)
    qh_ref[1] = jnp.where(feat >= HEAD_DIM, qt, zero)

    units = [(c, u) for u in range(tq // QCHUNK) for c in range(2)]

    def scores(start, nk, c, u):
        kb = k_ref[pl.ds(start, nk), :]
        qc = qh_ref[c, :, pl.ds(u * QCHUNK, QCHUNK)]
        return jnp.dot(kb, qc, preferred_element_type=F32)

    def block(start, *, diagonal, mode, carried=False):
        def n_keys(u):
            return (u + 1) * QCHUNK if diagonal else tk

        if diagonal:
            tri = (lax.broadcasted_iota(jnp.int32, (QCHUNK, QCHUNK), 0)
                   <= lax.broadcasted_iota(jnp.int32, (QCHUNK, QCHUNK), 1))

        st_next = st_ref[...] if carried else scores(start, n_keys(units[0][1]), *units[0])
        for idx, (c, u) in enumerate(units):
            st = st_next
            if idx + 1 < len(units):
                c_next, u_next = units[idx + 1]
                st_next = scores(start, n_keys(u_next), c_next, u_next)
            elif carried:
                st_ref[...] = scores(pl.multiple_of(start + tk, tk), tk, *units[0])
            nk = n_keys(u)
            cols = pl.ds(u * QCHUNK, QCHUNK)
            if diagonal:
                last = jnp.where(tri, st[nk - QCHUNK:], -jnp.inf)
                st = last if nk == QCHUNK else jnp.concatenate([st[:nk - QCHUNK], last], axis=0)
            vt = vt_ref[:, pl.ds(start, nk)]
            if mode == "online":
                m_old = m_ref[c, :, cols]
                m_new = jnp.maximum(m_old, jnp.max(st, axis=0, keepdims=True))
                p = jnp.exp2(st - m_new)
                alpha = jnp.exp2(m_old - m_new)
                l_ref[c, :, cols] = alpha * l_ref[c, :, cols] + jnp.sum(p, axis=0, keepdims=True)
                m_ref[c, :, cols] = m_new
                pv = jnp.dot(vt, p.astype(BF16), preferred_element_type=F32)
                acc_ref[c, :, cols] = alpha * acc_ref[c, :, cols] + pv
            elif mode == "fixed":
                p = jnp.exp2(st - m_ref[c, :, cols])
                l_ref[c, :, cols] += jnp.sum(p, axis=0, keepdims=True)
                acc_ref[c, :, cols] += jnp.dot(vt, p.astype(BF16), preferred_element_type=F32)
            else:
                m = jnp.max(last, axis=0, keepdims=True)
                p = jnp.exp2(st - m)
                m_ref[c, :, cols] = m
                l_ref[c, :, cols] = jnp.sum(p, axis=0, keepdims=True)
                acc_ref[c, :, cols] = jnp.dot(vt, p.astype(BF16), preferred_element_type=F32)

    def all_blocks(online):
        if online:
            m_ref[...] = jnp.full(m_ref.shape, -jnp.inf, F32)
            l_ref[...] = jnp.zeros(l_ref.shape, F32)
            acc_ref[...] = jnp.zeros(acc_ref.shape, F32)
        block(pl.multiple_of(qi * tq, tq), diagonal=True, mode="online" if online else "start")
        st_ref[...] = scores(0, tk, *units[0])

        def body(j, carry):
            block(pl.multiple_of(j * tk, tk), diagonal=False,
                  mode="online" if online else "fixed", carried=True)
            return carry

        lax.fori_loop(0, qi * (tq // tk), body, 0)

    lq1, lk1, lq2, lk2 = (lam_ref[pl.ds(r, 1), :] for r in range(4))
    lam = (jnp.exp(jnp.sum(lq1 * lk1, axis=-1, keepdims=True))
           - jnp.exp(jnp.sum(lq2 * lk2, axis=-1, keepdims=True)) + lambda_init)

    def finish():
        o = acc_ref[0] / l_ref[0] - lam * (acc_ref[1] / l_ref[1])
        ms = jnp.mean(o * o, axis=0, keepdims=True)
        on = o * lax.rsqrt(ms + EPS) * g_ref[...] * (1.0 - lambda_init)
        o_ref[...] = on.T.astype(BF16)
        return ms

    all_blocks(online=False)
    ms = finish()
    overflow_probe = jnp.sum(ms * 0.0) + jnp.sum(l_ref[...] * 0.0)

    @pl.when(jnp.logical_not(overflow_probe == 0.0))
    def _():
        all_blocks(online=True)
        finish()


def _attention(z, lam_params, subln_g, layer, *, tq, tk, lambda_init):
    s = z.shape[1]
    kern = functools.partial(_attn_kernel, tq=tq, tk=tk, lambda_init=lambda_init)
    return pl.pallas_call(
        kern,
        grid=(N_HEADS, s // tq),
        in_specs=[
            _layer_block(lam_params, layer),
            _layer_block(subln_g, layer),
            pl.BlockSpec((None, tq, V_DIM), lambda h, i: (h, i, 0)),
            pl.BlockSpec((None, s, V_DIM), lambda h, i: (OFF_K // V_DIM + h, 0, 0)),
            pl.BlockSpec((None, s, V_DIM), lambda h, i: (OFF_V // V_DIM + h, 0, 0)),
        ],
        out_specs=pl.BlockSpec((tq, V_DIM), lambda h, i: (i, h)),
        out_shape=jax.ShapeDtypeStruct((s, ATTN_WIDTH), BF16),
        scratch_shapes=[
            pltpu.VMEM((V_DIM, s), BF16),
            pltpu.VMEM((2, V_DIM, tq), BF16),
            pltpu.VMEM((tk, QCHUNK), F32),
            pltpu.VMEM((2, 1, tq), F32),
            pltpu.VMEM((2, 1, tq), F32),
            pltpu.VMEM((2, V_DIM, tq), F32),
        ],
        compiler_params=pltpu.CompilerParams(
            dimension_semantics=("parallel", "arbitrary"),
            vmem_limit_bytes=VMEM_LIMIT_BYTES),
        name="diff_attn",
    )(lam_params, subln_g, z, z, z)


def _merge_kernel(x_ref, ya_ref, cb_ref, cc_ref, ch_ref, pu_ref, ga_ref, gb_ref, gc_ref,
                  cch_ref, chh_ref, puh_ref,
                  bg_ref, cw_ref, pw_ref, ps_ref, pa_ref, pb_ref, pc_ref, wo_ref, gp_ref,
                  o_ref, ext_ref, *, tm):
    i = pl.program_id(0)
    not_first = (i > 0).astype(F32)

    def wide(ref):
        return jnp.concatenate([ref[c] for c in range(ref.shape[0])], axis=-1).astype(F32)

    u = wide(cc_ref) * wide(ch_ref)
    ext_ref[pl.ds(0, HALO), :] = wide(cch_ref) * wide(chh_ref) * not_first
    ext_ref[pl.ds(HALO, tm), :] = u
    conv = u * cw_ref[pl.ds(CONV_K - 1, 1), :]
    for j in range(CONV_K - 1):
        back = CONV_K - 1 - j
        conv = conv + ext_ref[pl.ds(HALO - back, tm), :] * cw_ref[pl.ds(j, 1), :]
    y_conv = (wide(cb_ref) * conv).astype(BF16)

    pu = wide(pu_ref)
    ext_ref[pl.ds(0, HALO), :] = wide(puh_ref) * not_first
    ext_ref[pl.ds(HALO, tm), :] = pu
    t1 = (i * tm + 1 + lax.broadcasted_iota(jnp.int32, (tm, 1), 0)).astype(F32)
    groups = []
    for gi, w in enumerate(POOL_WINDOWS):
        cols = pl.ds(gi * POOL_GROUP, POOL_GROUP)
        ug = pu[:, gi * POOL_GROUP:(gi + 1) * POOL_GROUP]
        win = ug
        for back in range(1, w):
            win = win + ext_ref[pl.ds(HALO - back, tm), cols]
        dgrp = win / jnp.minimum(t1, float(w)) - ug
        groups.append(jnp.dot(dgrp.astype(BF16), pw_ref[gi], preferred_element_type=F32))
    y_pool = (jnp.concatenate(groups, axis=-1) * ps_ref[...]).astype(BF16)

    bg = bg_ref[...]
    d = x_ref.shape[-1]
    merged = (jax.nn.sigmoid(wide(ga_ref) + bg[:, 0:d])
              * jnp.dot(ya_ref[...], pa_ref[...], preferred_element_type=F32))
    merged += (jax.nn.sigmoid(wide(gb_ref) + bg[:, d:2 * d])
               * jnp.dot(y_conv, pb_ref[...], preferred_element_type=F32))
    merged += (jax.nn.sigmoid(wide(gc_ref) + bg[:, 2 * d:3 * d])
               * jnp.dot(y_pool, pc_ref[...], preferred_element_type=F32))
    mix = jnp.dot(merged.astype(BF16), wo_ref[...], preferred_element_type=F32)
    o_ref[...] = x_ref[...] + _rms(mix, gp_ref[...])


def _merge(x, z, ya, bg, cw, pw, ps, pa, pb, pc, wo, gp, layer, *, tm):
    s, d = x.shape
    hb = tm // HALO

    def col(width, off):
        return pl.BlockSpec((width // LANES, tm, LANES), lambda i: (off // width, i, 0))

    def halo(off):
        return pl.BlockSpec((CONV_WIDTH // LANES, HALO, LANES),
                            lambda i: (off // CONV_WIDTH, jnp.maximum(i * hb - 1, 0), 0))

    def whole(a):
        return _layer_block(a, layer)

    kern = functools.partial(_merge_kernel, tm=tm)
    return pl.pallas_call(
        kern,
        grid=(s // tm,),
        in_specs=[
            pl.BlockSpec((tm, d), lambda i: (i, 0)),
            pl.BlockSpec((tm, ATTN_WIDTH), lambda i: (i, 0)),
            col(CONV_WIDTH, OFF_CB), col(CONV_WIDTH, OFF_CC), col(CONV_WIDTH, OFF_CH),
            col(POOL_WIDTH, OFF_POOL),
            col(d, OFF_GATE), col(d, OFF_GATE + d), col(d, OFF_GATE + 2 * d),
            halo(OFF_CC), halo(OFF_CH), halo(OFF_POOL),
            whole(bg), whole(cw), whole(pw), whole(ps), whole(pa), whole(pb), whole(pc),
            whole(wo), whole(gp),
        ],
        out_specs=pl.BlockSpec((tm, d), lambda i: (i, 0)),
        out_shape=jax.ShapeDtypeStruct((s, d), F32),
        scratch_shapes=[pltpu.VMEM((tm + HALO, CONV_WIDTH), F32)],
        compiler_params=pltpu.CompilerParams(
            dimension_semantics=("parallel",),
            vmem_limit_bytes=VMEM_LIMIT_BYTES),
        name="merge",
    )(x, ya, z, z, z, z, z, z, z, z, z, z, bg, cw, pw, ps, pa, pb, pc, wo, gp)


def _ffn_kernel(x_ref, gpre_ref, wg_ref, wu_ref, wd_ref, gpost_ref, o_ref, hn_ref, acc_ref):
    c = pl.program_id(1)

    @pl.when(c == 0)
    def _():
        hn_ref[...] = _rms(x_ref[...], gpre_ref[...]).astype(BF16)
        acc_ref[...] = jnp.zeros(acc_ref.shape, F32)

    hn = hn_ref[...]
    gate = jnp.dot(hn, wg_ref[...], preferred_element_type=F32)
    up = jnp.dot(hn, wu_ref[...], preferred_element_type=F32)
    hf = (gate * jax.nn.sigmoid(gate) * up).astype(BF16)
    acc_ref[...] += jnp.dot(hf, wd_ref[...], preferred_element_type=F32)

    @pl.when(c == pl.num_programs(1) - 1)
    def _():
        o_ref[...] = x_ref[...] + _rms(acc_ref[...], gpost_ref[...])


def _ffn(x, gpre, wgu, wd, gpost, layer, *, tm, fc):
    s, d = x.shape
    d_ff = wd.shape[1]
    nc = d_ff // fc
    return pl.pallas_call(
        _ffn_kernel,
        grid=(s // tm, nc),
        in_specs=[
            pl.BlockSpec((tm, d), lambda i, c: (i, 0)),
            _layer_block(gpre, layer),
            _layer_block(wgu, layer, (d, fc), lambda i, c: (0, c)),
            _layer_block(wgu, layer, (d, fc), lambda i, c: (0, nc + c)),
            _layer_block(wd, layer, (fc, d), lambda i, c: (c, 0)),
            _layer_block(gpost, layer),
        ],
        out_specs=pl.BlockSpec((tm, d), lambda i, c: (i, 0)),
        out_shape=jax.ShapeDtypeStruct((s, d), F32),
        scratch_shapes=[pltpu.VMEM((tm, d), BF16), pltpu.VMEM((tm, d), F32)],
        compiler_params=pltpu.CompilerParams(
            dimension_semantics=("parallel", "arbitrary"),
            vmem_limit_bytes=VMEM_LIMIT_BYTES),
        name="ffn",
    )(x, gpre, wgu, wgu, wd, gpost)


def _tiles(s):
    pick = lambda pref: min(pref, s)
    return dict(proj_tm=pick(1024), proj_tn=2048, tq=pick(2048), tk=pick(1024),
                merge_tm=pick(512), ffn_tm=pick(1024))


def kernel(x, g_pre_mix, w_in, b_gate, lambda_q1, lambda_k1, lambda_q2, lambda_k2, subln_g,
           w_attn_proj, conv_w, w_conv_proj, pool_w, pool_scale, w_pool_proj, w_out,
           g_post_mix, g_pre_ffn, w_gate_up, w_down, g_post_ffn):
    b, s, d = x.shape
    depth = w_in.shape[0]
    d_ff = w_down.shape[1]
    t = _tiles(s)
    rows = lambda a: a.reshape(depth, 1, -1)
    bf = lambda a: a.astype(BF16)
    g_pre_mix, b_gate, pool_scale, g_post_mix, g_pre_ffn, g_post_ffn = map(
        rows, (g_pre_mix, b_gate, pool_scale, g_post_mix, g_pre_ffn, g_post_ffn))
    w_in, pool_w, w_attn_proj, w_conv_proj, w_pool_proj, w_out, w_gate_up, w_down = map(
        bf, (w_in, pool_w, w_attn_proj, w_conv_proj, w_pool_proj, w_out, w_gate_up, w_down))
    lam_params = jnp.stack([lambda_q1, lambda_k1, lambda_q2, lambda_k2], axis=1).astype(F32)
    subln_col = subln_g.reshape(depth, V_DIM, 1)
    outs = []
    for bi in range(b):
        h = x[bi]
        for l in range(depth):
            lambda_init = 0.8 - 0.6 * math.exp(-0.3 * l)
            z = _in_proj(h, g_pre_mix, w_in, l, tm=t["proj_tm"], tn=t["proj_tn"])
            tq = t["tq"] if l % 2 else t["tk"]
            ya = _attention(z, lam_params, subln_col, l, tq=tq, tk=t["tk"], lambda_init=lambda_init)
            h = _merge(h, z, ya, b_gate, conv_w, pool_w, pool_scale, w_attn_proj, w_conv_proj,
                       w_pool_proj, w_out, g_post_mix, l, tm=t["merge_tm"])
            h = _ffn(h, g_pre_ffn, w_gate_up, w_down, g_post_ffn, l, tm=t["ffn_tm"], fc=d_ff // 2)
        outs.append(h)
    return jnp.stack(outs)
```

```python
import functools
import math

import jax
import jax.numpy as jnp
from jax import lax
from jax.experimental import pallas as pl
from jax.experimental.pallas import tpu as pltpu

F32 = jnp.float32
BF16 = jnp.bfloat16
LANES = 128

EPS = 1e-6
N_HEADS = 8
HEAD_DIM = 64
V_DIM = 2 * HEAD_DIM
ATTN_WIDTH = N_HEADS * V_DIM
CONV_WIDTH = 512
CONV_K = 3
POOL_WINDOWS = (2, 4, 8, 16)
POOL_GROUP = 128
POOL_WIDTH = POOL_GROUP * len(POOL_WINDOWS)
HALO = 16
QCHUNK = 256
VT_CHUNK = 512
LOG2_E = math.log2(math.e)

OFF_K = ATTN_WIDTH
OFF_V = 2 * ATTN_WIDTH
OFF_CB = 3 * ATTN_WIDTH
OFF_CC = OFF_CB + CONV_WIDTH
OFF_CH = OFF_CC + CONV_WIDTH
OFF_POOL = OFF_CH + CONV_WIDTH
OFF_GATE = OFF_POOL + POOL_WIDTH

VMEM_LIMIT_BYTES = 56 * 1024 * 1024


def _layer_block(a, layer, block=None, index=None):
    rest = a.shape[1:] if block is None else block
    if index is None:
        index = lambda *grid: (0,) * len(rest)
    return pl.BlockSpec((None,) + tuple(rest), lambda *grid: (layer,) + tuple(index(*grid)))


def _sigmoid(x):
    return 0.5 * jnp.tanh(0.5 * x) + 0.5


def _rms(x, g):
    return x * lax.rsqrt(jnp.mean(x * x, axis=-1, keepdims=True) + EPS) * g


def _inproj_kernel(x_ref, g_ref, w_ref, z_ref, xn_ref):
    @pl.when(pl.program_id(1) == 0)
    def _():
        xn_ref[...] = _rms(x_ref[...], g_ref[...]).astype(BF16)

    z = jnp.dot(xn_ref[...], w_ref[...], preferred_element_type=F32).astype(BF16)
    for c in range(z_ref.shape[0]):
        z_ref[c] = z[:, c * LANES:(c + 1) * LANES]


def _in_proj(x, g, w, layer, *, tm, tn):
    s, d = x.shape
    n = w.shape[-1]
    return pl.pallas_call(
        _inproj_kernel,
        grid=(s // tm, n // tn),
        in_specs=[
            pl.BlockSpec((tm, d), lambda i, j: (i, 0)),
            _layer_block(g, layer),
            _layer_block(w, layer, (d, tn), lambda i, j: (0, j)),
        ],
        out_specs=pl.BlockSpec((tn // LANES, tm, LANES), lambda i, j: (j, i, 0)),
        out_shape=jax.ShapeDtypeStruct((n // LANES, s, LANES), BF16),
        scratch_shapes=[pltpu.VMEM((tm, d), BF16)],
        compiler_params=pltpu.CompilerParams(
            dimension_semantics=("parallel", "arbitrary"),
            vmem_limit_bytes=VMEM_LIMIT_BYTES),
        name="in_proj",
    )(x, g, w)


def _attn_kernel(lam_ref, g_ref, q_ref, k_ref, v_ref, o_ref,
                 vt_ref, qh_ref, st_ref, m_ref, l_ref, acc_ref, *, tq, tk, lambda_init):
    qi = pl.program_id(1)
    s_len = k_ref.shape[0]

    @pl.when(qi == 0)
    def _():
        for r in range(0, s_len, VT_CHUNK):
            vt_ref[:, pl.ds(r, VT_CHUNK)] = v_ref[pl.ds(r, VT_CHUNK), :].T

    q = (q_ref[...].astype(F32) * (HEAD_DIM ** -0.5 * LOG2_E)).astype(BF16)
    qt = q.T
    feat = lax.broadcasted_iota(jnp.int32, (V_DIM, tq), 0)
    zero = jnp.zeros_like(qt)
    qh_ref[0] = jnp.where(feat < HEAD_DIM, qt, zero)
    qh_ref[1] = jnp.where(feat >= HEAD_DIM, qt, zero)

    units = [(c, u) for u in range(tq // QCHUNK) for c in range(2)]

    def scores(start, nk, c, u):
        kb = k_ref[pl.ds(start, nk), :]
        qc = qh_ref[c, :, pl.ds(u * QCHUNK, QCHUNK)]
        return jnp.dot(kb, qc, preferred_element_type=F32)

    def run_units(start, todo, *, key_off, n_keys, mode, masked=False, carried=False):
        def first_key(u):
            return pl.multiple_of(start + key_off(u), QCHUNK)

        if masked:
            tri = (lax.broadcasted_iota(jnp.int32, (QCHUNK, QCHUNK), 0)
                   <= lax.broadcasted_iota(jnp.int32, (QCHUNK, QCHUNK), 1))

        c0, u0 = todo[0]
        st_next = st_ref[...] if carried else scores(first_key(u0), n_keys(u0), c0, u0)
        for idx, (c, u) in enumerate(todo):
            st = st_next
            if idx + 1 < len(todo):
                c_next, u_next = todo[idx + 1]
                st_next = scores(first_key(u_next), n_keys(u_next), c_next, u_next)
            elif carried:
                st_ref[...] = scores(pl.multiple_of(start + tk, tk), tk, *units[0])
            nk = n_keys(u)
            cols = pl.ds(u * QCHUNK, QCHUNK)
            if masked:
                last = jnp.where(tri, st[nk - QCHUNK:], -jnp.inf)
                st = last if nk == QCHUNK else jnp.concatenate([st[:nk - QCHUNK], last], axis=0)
            vt = vt_ref[:, pl.ds(first_key(u), nk)]
            if mode == "online":
                m_old = m_ref[c, :, cols]
                m_new = jnp.maximum(m_old, jnp.max(st, axis=0, keepdims=True))
                p = jnp.exp2(st - m_new)
                alpha = jnp.exp2(m_old - m_new)
                l_ref[c, :, cols] = alpha * l_ref[c, :, cols] + jnp.sum(p, axis=0, keepdims=True)
                m_ref[c, :, cols] = m_new
                pv = jnp.dot(vt, p.astype(BF16), preferred_element_type=F32)
                acc_ref[c, :, cols] = alpha * acc_ref[c, :, cols] + pv
            elif mode == "fixed":
                p = jnp.exp2(st - m_ref[c, :, cols])
                l_ref[c, :, cols] += jnp.sum(p, axis=0, keepdims=True)
                acc_ref[c, :, cols] += jnp.dot(vt, p.astype(BF16), preferred_element_type=F32)
            else:
                m = jnp.max(st, axis=0, keepdims=True)
                p = jnp.exp2(st - m)
                m_ref[c, :, cols] = m
                l_ref[c, :, cols] = jnp.sum(p, axis=0, keepdims=True)
                acc_ref[c, :, cols] = jnp.dot(vt, p.astype(BF16), preferred_element_type=F32)

    def all_blocks(online):
        diag = pl.multiple_of(qi * tq, tq)
        if online:
            m_ref[...] = jnp.full(m_ref.shape, -jnp.inf, F32)
            l_ref[...] = jnp.zeros(l_ref.shape, F32)
            acc_ref[...] = jnp.zeros(acc_ref.shape, F32)
            run_units(diag, units, key_off=lambda u: 0, n_keys=lambda u: (u + 1) * QCHUNK,
                      mode="online", masked=True)
        else:
            run_units(diag, units, key_off=lambda u: u * QCHUNK, n_keys=lambda u: QCHUNK,
                      mode="start", masked=True)
            run_units(diag, [cu for cu in units if cu[1] > 0], key_off=lambda u: 0,
                      n_keys=lambda u: u * QCHUNK, mode="fixed")
        st_ref[...] = scores(0, tk, *units[0])

        def body(j, carry):
            run_units(pl.multiple_of(j * tk, tk), units, key_off=lambda u: 0, n_keys=lambda u: tk,
                      mode="online" if online else "fixed", carried=True)
            return carry

        lax.fori_loop(0, qi * (tq // tk), body, 0)

    lq1, lk1, lq2, lk2 = (lam_ref[pl.ds(r, 1), :] for r in range(4))
    lam = (jnp.exp(jnp.sum(lq1 * lk1, axis=-1, keepdims=True))
           - jnp.exp(jnp.sum(lq2 * lk2, axis=-1, keepdims=True)) + lambda_init)

    def finish():
        o = acc_ref[0] / l_ref[0] - lam * (acc_ref[1] / l_ref[1])
        ms = jnp.mean(o * o, axis=0, keepdims=True)
        on = o * lax.rsqrt(ms + EPS) * g_ref[...] * (1.0 - lambda_init)
        o_ref[...] = on.T.astype(BF16)
        return ms

    all_blocks(online=False)
    ms = finish()
    overflow_probe = jnp.sum(ms * 0.0) + jnp.sum(l_ref[...] * 0.0)

    @pl.when(jnp.logical_not(overflow_probe == 0.0))
    def _():
        all_blocks(online=True)
        finish()


def _attention(z, lam_params, subln_g, layer, *, tq, tk, lambda_init):
    s = z.shape[1]
    kern = functools.partial(_attn_kernel, tq=tq, tk=tk, lambda_init=lambda_init)
    return pl.pallas_call(
        kern,
        grid=(N_HEADS, s // tq),
        in_specs=[
            _layer_block(lam_params, layer),
            _layer_block(subln_g, layer),
            pl.BlockSpec((None, tq, V_DIM), lambda h, i: (h, i, 0)),
            pl.BlockSpec((None, s, V_DIM), lambda h, i: (OFF_K // V_DIM + h, 0, 0)),
            pl.BlockSpec((None, s, V_DIM), lambda h, i: (OFF_V // V_DIM + h, 0, 0)),
        ],
        out_specs=pl.BlockSpec((tq, V_DIM), lambda h, i: (i, h)),
        out_shape=jax.ShapeDtypeStruct((s, ATTN_WIDTH), BF16),
        scratch_shapes=[
            pltpu.VMEM((V_DIM, s), BF16),
            pltpu.VMEM((2, V_DIM, tq), BF16),
            pltpu.VMEM((tk, QCHUNK), F32),
            pltpu.VMEM((2, 1, tq), F32),
            pltpu.VMEM((2, 1, tq), F32),
            pltpu.VMEM((2, V_DIM, tq), F32),
        ],
        compiler_params=pltpu.CompilerParams(
            dimension_semantics=("parallel", "arbitrary"),
            vmem_limit_bytes=VMEM_LIMIT_BYTES),
        name="diff_attn",
    )(lam_params, subln_g, z, z, z)


def _merge_kernel(x_ref, ya_ref, cb_ref, cc_ref, ch_ref, pu_ref, ga_ref, gb_ref, gc_ref,
                  cch_ref, chh_ref, puh_ref,
                  bg_ref, cw_ref, pw_ref, ps_ref, pa_ref, pb_ref, pc_ref, wo_ref, gp_ref,
                  o_ref, ext_ref, *, tm):
    i = pl.program_id(0)
    not_first = (i > 0).astype(F32)

    def wide(ref):
        return jnp.concatenate([ref[c] for c in range(ref.shape[0])], axis=-1).astype(F32)

    u = wide(cc_ref) * wide(ch_ref)
    ext_ref[pl.ds(0, HALO), :] = wide(cch_ref) * wide(chh_ref) * not_first
    ext_ref[pl.ds(HALO, tm), :] = u
    conv = u * cw_ref[pl.ds(CONV_K - 1, 1), :]
    for j in range(CONV_K - 1):
        back = CONV_K - 1 - j
        conv = conv + ext_ref[pl.ds(HALO - back, tm), :] * cw_ref[pl.ds(j, 1), :]
    y_conv = (wide(cb_ref) * conv).astype(BF16)

    pu = wide(pu_ref)
    ext_ref[pl.ds(0, HALO), :] = wide(puh_ref) * not_first
    ext_ref[pl.ds(HALO, tm), :] = pu
    t1 = (i * tm + 1 + lax.broadcasted_iota(jnp.int32, (tm, 1), 0)).astype(F32)
    groups = []
    for gi, w in enumerate(POOL_WINDOWS):
        cols = pl.ds(gi * POOL_GROUP, POOL_GROUP)
        ug = pu[:, gi * POOL_GROUP:(gi + 1) * POOL_GROUP]
        win = ug
        for back in range(1, w):
            win = win + ext_ref[pl.ds(HALO - back, tm), cols]
        dgrp = win / jnp.minimum(t1, float(w)) - ug
        groups.append(jnp.dot(dgrp.astype(BF16), pw_ref[gi], preferred_element_type=F32))
    y_pool = (jnp.concatenate(groups, axis=-1) * ps_ref[...]).astype(BF16)

    bg = bg_ref[...]
    d = x_ref.shape[-1]
    merged = (_sigmoid(wide(ga_ref) + bg[:, 0:d])
              * jnp.dot(ya_ref[...], pa_ref[...], preferred_element_type=F32))
    merged += (_sigmoid(wide(gb_ref) + bg[:, d:2 * d])
               * jnp.dot(y_conv, pb_ref[...], preferred_element_type=F32))
    merged += (_sigmoid(wide(gc_ref) + bg[:, 2 * d:3 * d])
               * jnp.dot(y_pool, pc_ref[...], preferred_element_type=F32))
    mix = jnp.dot(merged.astype(BF16), wo_ref[...], preferred_element_type=F32)
    o_ref[...] = x_ref[...] + _rms(mix, gp_ref[...])


def _merge(x, z, ya, bg, cw, pw, ps, pa, pb, pc, wo, gp, layer, *, tm):
    s, d = x.shape
    hb = tm // HALO

    def col(width, off):
        return pl.BlockSpec((width // LANES, tm, LANES), lambda i: (off // width, i, 0))

    def halo(off):
        return pl.BlockSpec((CONV_WIDTH // LANES, HALO, LANES),
                            lambda i: (off // CONV_WIDTH, jnp.maximum(i * hb - 1, 0), 0))

    def whole(a):
        return _layer_block(a, layer)

    kern = functools.partial(_merge_kernel, tm=tm)
    return pl.pallas_call(
        kern,
        grid=(s // tm,),
        in_specs=[
            pl.BlockSpec((tm, d), lambda i: (i, 0)),
            pl.BlockSpec((tm, ATTN_WIDTH), lambda i: (i, 0)),
            col(CONV_WIDTH, OFF_CB), col(CONV_WIDTH, OFF_CC), col(CONV_WIDTH, OFF_CH),
            col(POOL_WIDTH, OFF_POOL),
            col(d, OFF_GATE), col(d, OFF_GATE + d), col(d, OFF_GATE + 2 * d),
            halo(OFF_CC), halo(OFF_CH), halo(OFF_POOL),
            whole(bg), whole(cw), whole(pw), whole(ps), whole(pa), whole(pb), whole(pc),
            whole(wo), whole(gp),
        ],
        out_specs=pl.BlockSpec((tm, d), lambda i: (i, 0)),
        out_shape=jax.ShapeDtypeStruct((s, d), F32),
        scratch_shapes=[pltpu.VMEM((tm + HALO, CONV_WIDTH), F32)],
        compiler_params=pltpu.CompilerParams(
            dimension_semantics=("parallel",),
            vmem_limit_bytes=VMEM_LIMIT_BYTES),
        name="merge",
    )(x, ya, z, z, z, z, z, z, z, z, z, z, bg, cw, pw, ps, pa, pb, pc, wo, gp)


def _ffn_kernel(x_ref, gpre_ref, wg_ref, wu_ref, wd_ref, gpost_ref, o_ref, hn_ref, acc_ref):
    c = pl.program_id(1)

    @pl.when(c == 0)
    def _():
        hn_ref[...] = _rms(x_ref[...], gpre_ref[...]).astype(BF16)
        acc_ref[...] = jnp.zeros(acc_ref.shape, F32)

    hn = hn_ref[...]
    gate = jnp.dot(hn, wg_ref[...], preferred_element_type=F32)
    up = jnp.dot(hn, wu_ref[...], preferred_element_type=F32)
    hf = (gate * _sigmoid(gate) * up).astype(BF16)
    acc_ref[...] += jnp.dot(hf, wd_ref[...], preferred_element_type=F32)

    @pl.when(c == pl.num_programs(1) - 1)
    def _():
        o_ref[...] = x_ref[...] + _rms(acc_ref[...], gpost_ref[...])


def _ffn(x, gpre, wgu, wd, gpost, layer, *, tm, fc):
    s, d = x.shape
    d_ff = wd.shape[1]
    nc = d_ff // fc
    return pl.pallas_call(
        _ffn_kernel,
        grid=(s // tm, nc),
        in_specs=[
            pl.BlockSpec((tm, d), lambda i, c: (i, 0)),
            _layer_block(gpre, layer),
            _layer_block(wgu, layer, (d, fc), lambda i, c: (0, c)),
            _layer_block(wgu, layer, (d, fc), lambda i, c: (0, nc + c)),
            _layer_block(wd, layer, (fc, d), lambda i, c: (c, 0)),
            _layer_block(gpost, layer),
        ],
        out_specs=pl.BlockSpec((tm, d), lambda i, c: (i, 0)),
        out_shape=jax.ShapeDtypeStruct((s, d), F32),
        scratch_shapes=[pltpu.VMEM((tm, d), BF16), pltpu.VMEM((tm, d), F32)],
        compiler_params=pltpu.CompilerParams(
            dimension_semantics=("parallel", "arbitrary"),
            vmem_limit_bytes=VMEM_LIMIT_BYTES),
        name="ffn",
    )(x, gpre, wgu, wgu, wd, gpost)


def _tiles(s):
    pick = lambda pref: min(pref, s)
    return dict(proj_tm=pick(1024), proj_tn=2048, tq=pick(2048), tk=pick(1024),
                merge_tm=pick(512), ffn_tm=pick(1024))


def kernel(x, g_pre_mix, w_in, b_gate, lambda_q1, lambda_k1, lambda_q2, lambda_k2, subln_g,
           w_attn_proj, conv_w, w_conv_proj, pool_w, pool_scale, w_pool_proj, w_out,
           g_post_mix, g_pre_ffn, w_gate_up, w_down, g_post_ffn):
    b, s, d = x.shape
    depth = w_in.shape[0]
    d_ff = w_down.shape[1]
    t = _tiles(s)
    rows = lambda a: a.reshape(depth, 1, -1)
    bf = lambda a: a.astype(BF16)
    g_pre_mix, b_gate, pool_scale, g_post_mix, g_pre_ffn, g_post_ffn = map(
        rows, (g_pre_mix, b_gate, pool_scale, g_post_mix, g_pre_ffn, g_post_ffn))
    w_in, pool_w, w_attn_proj, w_conv_proj, w_pool_proj, w_out, w_gate_up, w_down = map(
        bf, (w_in, pool_w, w_attn_proj, w_conv_proj, w_pool_proj, w_out, w_gate_up, w_down))
    lam_params = jnp.stack([lambda_q1, lambda_k1, lambda_q2, lambda_k2], axis=1).astype(F32)
    subln_col = subln_g.reshape(depth, V_DIM, 1)
    outs = []
    for bi in range(b):
        h = x[bi]
        for l in range(depth):
            lambda_init = 0.8 - 0.6 * math.exp(-0.3 * l)
            z = _in_proj(h, g_pre_mix, w_in, l, tm=t["proj_tm"], tn=t["proj_tn"])
            ya = _attention(z, lam_params, subln_col, l, tq=t["tq"], tk=t["tk"], lambda_init=lambda_init)
            h = _merge(h, z, ya, b_gate, conv_w, pool_w, pool_scale, w_attn_proj, w_conv_proj,
                       w_pool_proj, w_out, g_post_mix, l, tm=t["merge_tm"])
            h = _ffn(h, g_pre_ffn, w_gate_up, w_down, g_post_ffn, l, tm=t["ffn_tm"], fc=d_ff // 2)
        outs.append(h)
    return jnp.stack(outs)
```

```python
import functools
import math

import jax
import jax.numpy as jnp
from jax import lax
from jax.experimental import pallas as pl
from jax.experimental.pallas import tpu as pltpu

F32 = jnp.float32
BF16 = jnp.bfloat16
LANES = 128

EPS = 1e-6
N_HEADS = 8
HEAD_DIM = 64
V_DIM = 2 * HEAD_DIM
ATTN_WIDTH = N_HEADS * V_DIM
CONV_WIDTH = 512
CONV_K = 3
POOL_WINDOWS = (2, 4, 8, 16)
POOL_GROUP = 128
POOL_WIDTH = POOL_GROUP * len(POOL_WINDOWS)
HALO = 16
QCHUNK = 256
VT_CHUNK = 512
LOG2_E = math.log2(math.e)

OFF_K = ATTN_WIDTH
OFF_V = 2 * ATTN_WIDTH
OFF_CB = 3 * ATTN_WIDTH
OFF_CC = OFF_CB + CONV_WIDTH
OFF_CH = OFF_CC + CONV_WIDTH
OFF_POOL = OFF_CH + CONV_WIDTH
OFF_GATE = OFF_POOL + POOL_WIDTH

VMEM_LIMIT_BYTES = 56 * 1024 * 1024


def _layer_block(a, layer, block=None, index=None):
    rest = a.shape[1:] if block is None else block
    if index is None:
        index = lambda *grid: (0,) * len(rest)
    return pl.BlockSpec((None,) + tuple(rest), lambda *grid: (layer,) + tuple(index(*grid)))


def _sigmoid(x):
    return 0.5 * jnp.tanh(0.5 * x) + 0.5


def _rms(x, g):
    return x * lax.rsqrt(jnp.mean(x * x, axis=-1, keepdims=True) + EPS) * g


def _inproj_kernel(x_ref, g_ref, w_ref, z_ref, xn_ref):
    @pl.when(pl.program_id(1) == 0)
    def _():
        xn_ref[...] = _rms(x_ref[...], g_ref[...]).astype(BF16)

    z = jnp.dot(xn_ref[...], w_ref[...], preferred_element_type=F32).astype(BF16)
    for c in range(z_ref.shape[0]):
        z_ref[c] = z[:, c * LANES:(c + 1) * LANES]


def _in_proj(x, g, w, layer, *, tm, tn):
    s, d = x.shape
    n = w.shape[-1]
    return pl.pallas_call(
        _inproj_kernel,
        grid=(s // tm, n // tn),
        in_specs=[
            pl.BlockSpec((tm, d), lambda i, j: (i, 0)),
            _layer_block(g, layer),
            _layer_block(w, layer, (d, tn), lambda i, j: (0, j)),
        ],
        out_specs=pl.BlockSpec((tn // LANES, tm, LANES), lambda i, j: (j, i, 0)),
        out_shape=jax.ShapeDtypeStruct((n // LANES, s, LANES), BF16),
        scratch_shapes=[pltpu.VMEM((tm, d), BF16)],
        compiler_params=pltpu.CompilerParams(
            dimension_semantics=("parallel", "arbitrary"),
            vmem_limit_bytes=VMEM_LIMIT_BYTES),
        name="in_proj",
    )(x, g, w)


def _attn_kernel(lam_ref, g_ref, q_ref, k_ref, v_ref, o_ref,
                 vt_ref, qh_ref, st_ref, m_ref, l_ref, acc_ref, *, tq, tk, lambda_init):
    qi = pl.program_id(1)
    s_len = k_ref.shape[0]

    @pl.when(qi == 0)
    def _():
        for r in range(0, s_len, VT_CHUNK):
            vt_ref[:, pl.ds(r, VT_CHUNK)] = v_ref[pl.ds(r, VT_CHUNK), :].T

    q = (q_ref[...].astype(F32) * (HEAD_DIM ** -0.5 * LOG2_E)).astype(BF16)
    qt = q.T
    feat = lax.broadcasted_iota(jnp.int32, (V_DIM, tq), 0)
    zero = jnp.zeros_like(qt)
    qh_ref[0] = jnp.where(feat < HEAD_DIM, qt, zero)
    qh_ref[1] = jnp.where(feat >= HEAD_DIM, qt, zero)

    units = [(c, u) for u in range(tq // QCHUNK) for c in range(2)]

    def scores(start, nk, c, u):
        kb = k_ref[pl.ds(start, nk), :]
        qc = qh_ref[c, :, pl.ds(u * QCHUNK, QCHUNK)]
        return jnp.dot(kb, qc, preferred_element_type=F32)

    def run_units(start, n_keys, *, online, masked=False, carried=False):
        if masked:
            tri = (lax.broadcasted_iota(jnp.int32, (QCHUNK, QCHUNK), 0)
                   <= lax.broadcasted_iota(jnp.int32, (QCHUNK, QCHUNK), 1))

        c0, u0 = units[0]
        st_next = st_ref[...] if carried else scores(start, n_keys(u0), c0, u0)
        for idx, (c, u) in enumerate(units):
            st = st_next
            if idx + 1 < len(units):
                c_next, u_next = units[idx + 1]
                st_next = scores(start, n_keys(u_next), c_next, u_next)
            elif carried:
                st_ref[...] = scores(pl.multiple_of(start + tk, tk), tk, c0, u0)
            nk = n_keys(u)
            cols = pl.ds(u * QCHUNK, QCHUNK)
            if masked:
                last = jnp.where(tri, st[nk - QCHUNK:], -jnp.inf)
                st = last if nk == QCHUNK else jnp.concatenate([st[:nk - QCHUNK], last], axis=0)
            vt = vt_ref[:, pl.ds(start, nk)]
            if online:
                m_old = m_ref[c, :, cols]
                m_new = jnp.maximum(m_old, jnp.max(st, axis=0, keepdims=True))
                p = jnp.exp2(st - m_new)
                alpha = jnp.exp2(m_old - m_new)
                l_ref[c, :, cols] = alpha * l_ref[c, :, cols] + jnp.sum(p, axis=0, keepdims=True)
                m_ref[c, :, cols] = m_new
                pv = jnp.dot(vt, p.astype(BF16), preferred_element_type=F32)
                acc_ref[c, :, cols] = alpha * acc_ref[c, :, cols] + pv
            else:
                p = jnp.exp2(st - m_ref[c, :, cols])
                l_ref[c, :, cols] += jnp.sum(p, axis=0, keepdims=True)
                acc_ref[c, :, cols] += jnp.dot(vt, p.astype(BF16), preferred_element_type=F32)

    def all_blocks(online):
        diag = pl.multiple_of(qi * tq, tq)
        if online:
            m_ref[...] = jnp.full(m_ref.shape, -jnp.inf, F32)
        else:
            for r in range(0, tq, VT_CHUNK):
                kt = k_ref[pl.ds(pl.multiple_of(diag + r, VT_CHUNK), VT_CHUNK), :].T.astype(F32)
                for c in range(2):
                    qc = qh_ref[c, :, pl.ds(r, VT_CHUNK)].astype(F32)
                    m_ref[c, :, pl.ds(r, VT_CHUNK)] = jnp.sum(qc * kt, axis=0, keepdims=True)
        l_ref[...] = jnp.zeros(l_ref.shape, F32)
        acc_ref[...] = jnp.zeros(acc_ref.shape, F32)
        run_units(diag, lambda u: (u + 1) * QCHUNK, online=online, masked=True)
        st_ref[...] = scores(0, tk, *units[0])

        def body(j, carry):
            run_units(pl.multiple_of(j * tk, tk), lambda u: tk, online=online, carried=True)
            return carry

        lax.fori_loop(0, qi * (tq // tk), body, 0)

    lq1, lk1, lq2, lk2 = (lam_ref[pl.ds(r, 1), :] for r in range(4))
    lam = (jnp.exp(jnp.sum(lq1 * lk1, axis=-1, keepdims=True))
           - jnp.exp(jnp.sum(lq2 * lk2, axis=-1, keepdims=True)) + lambda_init)

    def finish():
        o = acc_ref[0] / l_ref[0] - lam * (acc_ref[1] / l_ref[1])
        ms = jnp.mean(o * o, axis=0, keepdims=True)
        on = o * lax.rsqrt(ms + EPS) * g_ref[...] * (1.0 - lambda_init)
        o_ref[...] = on.T.astype(BF16)
        return ms

    all_blocks(online=False)
    ms = finish()
    overflow_probe = jnp.sum(ms * 0.0) + jnp.sum(l_ref[...] * 0.0)

    @pl.when(jnp.logical_not(overflow_probe == 0.0))
    def _():
        all_blocks(online=True)
        finish()


def _attention(z, lam_params, subln_g, layer, *, tq, tk, lambda_init):
    s = z.shape[1]
    kern = functools.partial(_attn_kernel, tq=tq, tk=tk, lambda_init=lambda_init)
    return pl.pallas_call(
        kern,
        grid=(N_HEADS, s // tq),
        in_specs=[
            _layer_block(lam_params, layer),
            _layer_block(subln_g, layer),
            pl.BlockSpec((None, tq, V_DIM), lambda h, i: (h, i, 0)),
            pl.BlockSpec((None, s, V_DIM), lambda h, i: (OFF_K // V_DIM + h, 0, 0)),
            pl.BlockSpec((None, s, V_DIM), lambda h, i: (OFF_V // V_DIM + h, 0, 0)),
        ],
        out_specs=pl.BlockSpec((tq, V_DIM), lambda h, i: (i, h)),
        out_shape=jax.ShapeDtypeStruct((s, ATTN_WIDTH), BF16),
        scratch_shapes=[
            pltpu.VMEM((V_DIM, s), BF16),
            pltpu.VMEM((2, V_DIM, tq), BF16),
            pltpu.VMEM((tk, QCHUNK), F32),
            pltpu.VMEM((2, 1, tq), F32),
            pltpu.VMEM((2, 1, tq), F32),
            pltpu.VMEM((2, V_DIM, tq), F32),
        ],
        compiler_params=pltpu.CompilerParams(
            dimension_semantics=("parallel", "arbitrary"),
            vmem_limit_bytes=VMEM_LIMIT_BYTES),
        name="diff_attn",
    )(lam_params, subln_g, z, z, z)


def _merge_kernel(x_ref, ya_ref, cb_ref, cc_ref, ch_ref, pu_ref, ga_ref, gb_ref, gc_ref,
                  cch_ref, chh_ref, puh_ref,
                  bg_ref, cw_ref, pw_ref, ps_ref, pa_ref, pb_ref, pc_ref, wo_ref, gp_ref,
                  o_ref, ext_ref, *, tm):
    i = pl.program_id(0)
    not_first = (i > 0).astype(F32)

    def wide(ref):
        return jnp.concatenate([ref[c] for c in range(ref.shape[0])], axis=-1).astype(F32)

    u = wide(cc_ref) * wide(ch_ref)
    ext_ref[pl.ds(0, HALO), :] = wide(cch_ref) * wide(chh_ref) * not_first
    ext_ref[pl.ds(HALO, tm), :] = u
    conv = u * cw_ref[pl.ds(CONV_K - 1, 1), :]
    for j in range(CONV_K - 1):
        back = CONV_K - 1 - j
        conv = conv + ext_ref[pl.ds(HALO - back, tm), :] * cw_ref[pl.ds(j, 1), :]
    y_conv = (wide(cb_ref) * conv).astype(BF16)

    pu = wide(pu_ref)
    ext_ref[pl.ds(0, HALO), :] = wide(puh_ref) * not_first
    ext_ref[pl.ds(HALO, tm), :] = pu
    t1 = (i * tm + 1 + lax.broadcasted_iota(jnp.int32, (tm, 1), 0)).astype(F32)
    groups = []
    for gi, w in enumerate(POOL_WINDOWS):
        cols = pl.ds(gi * POOL_GROUP, POOL_GROUP)
        ug = pu[:, gi * POOL_GROUP:(gi + 1) * POOL_GROUP]
        win = ug
        for back in range(1, w):
            win = win + ext_ref[pl.ds(HALO - back, tm), cols]
        dgrp = win / jnp.minimum(t1, float(w)) - ug
        groups.append(jnp.dot(dgrp.astype(BF16), pw_ref[gi], preferred_element_type=F32))
    y_pool = (jnp.concatenate(groups, axis=-1) * ps_ref[...]).astype(BF16)

    bg = bg_ref[...]
    d = x_ref.shape[-1]
    merged = (_sigmoid(wide(ga_ref) + bg[:, 0:d])
              * jnp.dot(ya_ref[...], pa_ref[...], preferred_element_type=F32))
    merged += (_sigmoid(wide(gb_ref) + bg[:, d:2 * d])
               * jnp.dot(y_conv, pb_ref[...], preferred_element_type=F32))
    merged += (_sigmoid(wide(gc_ref) + bg[:, 2 * d:3 * d])
               * jnp.dot(y_pool, pc_ref[...], preferred_element_type=F32))
    mix = jnp.dot(merged.astype(BF16), wo_ref[...], preferred_element_type=F32)
    o_ref[...] = x_ref[...] + _rms(mix, gp_ref[...])


def _merge(x, z, ya, bg, cw, pw, ps, pa, pb, pc, wo, gp, layer, *, tm):
    s, d = x.shape
    hb = tm // HALO

    def col(width, off):
        return pl.BlockSpec((width // LANES, tm, LANES), lambda i: (off // width, i, 0))

    def halo(off):
        return pl.BlockSpec((CONV_WIDTH // LANES, HALO, LANES),
                            lambda i: (off // CONV_WIDTH, jnp.maximum(i * hb - 1, 0), 0))

    def whole(a):
        return _layer_block(a, layer)

    kern = functools.partial(_merge_kernel, tm=tm)
    return pl.pallas_call(
        kern,
        grid=(s // tm,),
        in_specs=[
            pl.BlockSpec((tm, d), lambda i: (i, 0)),
            pl.BlockSpec((tm, ATTN_WIDTH), lambda i: (i, 0)),
            col(CONV_WIDTH, OFF_CB), col(CONV_WIDTH, OFF_CC), col(CONV_WIDTH, OFF_CH),
            col(POOL_WIDTH, OFF_POOL),
            col(d, OFF_GATE), col(d, OFF_GATE + d), col(d, OFF_GATE + 2 * d),
            halo(OFF_CC), halo(OFF_CH), halo(OFF_POOL),
            whole(bg), whole(cw), whole(pw), whole(ps), whole(pa), whole(pb), whole(pc),
            whole(wo), whole(gp),
        ],
        out_specs=pl.BlockSpec((tm, d), lambda i: (i, 0)),
        out_shape=jax.ShapeDtypeStruct((s, d), F32),
        scratch_shapes=[pltpu.VMEM((tm + HALO, CONV_WIDTH), F32)],
        compiler_params=pltpu.CompilerParams(
            dimension_semantics=("parallel",),
            vmem_limit_bytes=VMEM_LIMIT_BYTES),
        name="merge",
    )(x, ya, z, z, z, z, z, z, z, z, z, z, bg, cw, pw, ps, pa, pb, pc, wo, gp)


def _ffn_kernel(x_ref, gpre_ref, wg_ref, wu_ref, wd_ref, gpost_ref, o_ref, hn_ref, acc_ref):
    c = pl.program_id(1)

    @pl.when(c == 0)
    def _():
        hn_ref[...] = _rms(x_ref[...], gpre_ref[...]).astype(BF16)
        acc_ref[...] = jnp.zeros(acc_ref.shape, F32)

    hn = hn_ref[...]
    gate = jnp.dot(hn, wg_ref[...], preferred_element_type=F32)
    up = jnp.dot(hn, wu_ref[...], preferred_element_type=F32)
    hf = (gate * _sigmoid(gate) * up).astype(BF16)
    acc_ref[...] += jnp.dot(hf, wd_ref[...], preferred_element_type=F32)

    @pl.when(c == pl.num_programs(1) - 1)
    def _():
        o_ref[...] = x_ref[...] + _rms(acc_ref[...], gpost_ref[...])


def _ffn(x, gpre, wgu, wd, gpost, layer, *, tm, fc):
    s, d = x.shape
    d_ff = wd.shape[1]
    nc = d_ff // fc
    return pl.pallas_call(
        _ffn_kernel,
        grid=(s // tm, nc),
        in_specs=[
            pl.BlockSpec((tm, d), lambda i, c: (i, 0)),
            _layer_block(gpre, layer),
            _layer_block(wgu, layer, (d, fc), lambda i, c: (0, c)),
            _layer_block(wgu, layer, (d, fc), lambda i, c: (0, nc + c)),
            _layer_block(wd, layer, (fc, d), lambda i, c: (c, 0)),
            _layer_block(gpost, layer),
        ],
        out_specs=pl.BlockSpec((tm, d), lambda i, c: (i, 0)),
        out_shape=jax.ShapeDtypeStruct((s, d), F32),
        scratch_shapes=[pltpu.VMEM((tm, d), BF16), pltpu.VMEM((tm, d), F32)],
        compiler_params=pltpu.CompilerParams(
            dimension_semantics=("parallel", "arbitrary"),
            vmem_limit_bytes=VMEM_LIMIT_BYTES),
        name="ffn",
    )(x, gpre, wgu, wgu, wd, gpost)


def _tiles(s):
    pick = lambda pref: min(pref, s)
    return dict(proj_tm=pick(1024), proj_tn=2048, tq=pick(2048), tk=pick(1024),
                merge_tm=pick(512), ffn_tm=pick(1024))


def kernel(x, g_pre_mix, w_in, b_gate, lambda_q1, lambda_k1, lambda_q2, lambda_k2, subln_g,
           w_attn_proj, conv_w, w_conv_proj, pool_w, pool_scale, w_pool_proj, w_out,
           g_post_mix, g_pre_ffn, w_gate_up, w_down, g_post_ffn):
    b, s, d = x.shape
    depth = w_in.shape[0]
    d_ff = w_down.shape[1]
    t = _tiles(s)
    rows = lambda a: a.reshape(depth, 1, -1)
    bf = lambda a: a.astype(BF16)
    g_pre_mix, b_gate, pool_scale, g_post_mix, g_pre_ffn, g_post_ffn = map(
        rows, (g_pre_mix, b_gate, pool_scale, g_post_mix, g_pre_ffn, g_post_ffn))
    w_in, pool_w, w_attn_proj, w_conv_proj, w_pool_proj, w_out, w_gate_up, w_down = map(
        bf, (w_in, pool_w, w_attn_proj, w_conv_proj, w_pool_proj, w_out, w_gate_up, w_down))
    lam_params = jnp.stack([lambda_q1, lambda_k1, lambda_q2, lambda_k2], axis=1).astype(F32)
    subln_col = subln_g.reshape(depth, V_DIM, 1)
    outs = []
    for bi in range(b):
        h = x[bi]
        for l in range(depth):
            lambda_init = 0.8 - 0.6 * math.exp(-0.3 * l)
            z = _in_proj(h, g_pre_mix, w_in, l, tm=t["proj_tm"], tn=t["proj_tn"])
            ya = _attention(z, lam_params, subln_col, l, tq=t["tq"], tk=t["tk"], lambda_init=lambda_init)
            h = _merge(h, z, ya, b_gate, conv_w, pool_w, pool_scale, w_attn_proj, w_conv_proj,
                       w_pool_proj, w_out, g_post_mix, l, tm=t["merge_tm"])
            h = _ffn(h, g_pre_ffn, w_gate_up, w_down, g_post_ffn, l, tm=t["ffn_tm"], fc=d_ff // 2)
        outs.append(h)
    return jnp.stack(outs)
```

```python
import functools
import math

import jax
import jax.numpy as jnp
from jax import lax
from jax.experimental import pallas as pl
from jax.experimental.pallas import tpu as pltpu

F32 = jnp.float32
BF16 = jnp.bfloat16
LANES = 128

EPS = 1e-6
N_HEADS = 8
HEAD_DIM = 64
V_DIM = 2 * HEAD_DIM
ATTN_WIDTH = N_HEADS * V_DIM
CONV_WIDTH = 512
CONV_K = 3
POOL_WINDOWS = (2, 4, 8, 16)
POOL_GROUP = 128
POOL_WIDTH = POOL_GROUP * len(POOL_WINDOWS)
HALO = 16
QCHUNK = 256
VT_CHUNK = 512
LOG2_E = math.log2(math.e)
ONLINE_BLOCK = 1024

OFF_K = ATTN_WIDTH
OFF_V = 2 * ATTN_WIDTH
OFF_CB = 3 * ATTN_WIDTH
OFF_CC = OFF_CB + CONV_WIDTH
OFF_CH = OFF_CC + CONV_WIDTH
OFF_POOL = OFF_CH + CONV_WIDTH
OFF_GATE = OFF_POOL + POOL_WIDTH

VMEM_LIMIT_BYTES = 56 * 1024 * 1024


def _layer_block(a, layer, block=None, index=None):
    rest = a.shape[1:] if block is None else block
    if index is None:
        index = lambda *grid: (0,) * len(rest)
    return pl.BlockSpec((None,) + tuple(rest), lambda *grid: (layer,) + tuple(index(*grid)))


def _sigmoid(x):
    return 0.5 * jnp.tanh(0.5 * x) + 0.5


def _rms(x, g):
    return x * lax.rsqrt(jnp.mean(x * x, axis=-1, keepdims=True) + EPS) * g


def _inproj_kernel(x_ref, g_ref, w_ref, z_ref, xn_ref):
    @pl.when(pl.program_id(1) == 0)
    def _():
        xn_ref[...] = _rms(x_ref[...], g_ref[...]).astype(BF16)

    z = jnp.dot(xn_ref[...], w_ref[...], preferred_element_type=F32).astype(BF16)
    for c in range(z_ref.shape[0]):
        z_ref[c] = z[:, c * LANES:(c + 1) * LANES]


def _in_proj(x, g, w, layer, *, tm, tn):
    s, d = x.shape
    n = w.shape[-1]
    return pl.pallas_call(
        _inproj_kernel,
        grid=(s // tm, n // tn),
        in_specs=[
            pl.BlockSpec((tm, d), lambda i, j: (i, 0)),
            _layer_block(g, layer),
            _layer_block(w, layer, (d, tn), lambda i, j: (0, j)),
        ],
        out_specs=pl.BlockSpec((tn // LANES, tm, LANES), lambda i, j: (j, i, 0)),
        out_shape=jax.ShapeDtypeStruct((n // LANES, s, LANES), BF16),
        scratch_shapes=[pltpu.VMEM((tm, d), BF16)],
        compiler_params=pltpu.CompilerParams(
            dimension_semantics=("parallel", "arbitrary"),
            vmem_limit_bytes=VMEM_LIMIT_BYTES),
        name="in_proj",
    )(x, g, w)


def _attn_kernel(lam_ref, g_ref, q_ref, k_ref, v_ref, o_ref,
                 vt_ref, qh_ref, st_ref, m_ref, l_ref, acc_ref, *, tq, tk, lambda_init):
    qi = pl.program_id(1)
    s_len = k_ref.shape[0]

    @pl.when(qi == 0)
    def _():
        for r in range(0, s_len, VT_CHUNK):
            vt_ref[:, pl.ds(r, VT_CHUNK)] = v_ref[pl.ds(r, VT_CHUNK), :].T

    q = (q_ref[...].astype(F32) * (HEAD_DIM ** -0.5 * LOG2_E)).astype(BF16)
    qt = q.T
    feat = lax.broadcasted_iota(jnp.int32, (V_DIM, tq), 0)
    zero = jnp.zeros_like(qt)
    qh_ref[0] = jnp.where(feat < HEAD_DIM, qt, zero)
    qh_ref[1] = jnp.where(feat >= HEAD_DIM, qt, zero)

    units = [(c, u) for u in range(tq // QCHUNK) for c in range(2)]

    def scores(start, nk, c, u):
        kb = k_ref[pl.ds(start, nk), :]
        qc = qh_ref[c, :, pl.ds(u * QCHUNK, QCHUNK)]
        return jnp.dot(kb, qc, preferred_element_type=F32)

    def run_units(start, n_keys, *, online, masked=False, carried=0):
        if masked:
            tri = (lax.broadcasted_iota(jnp.int32, (QCHUNK, QCHUNK), 0)
                   <= lax.broadcasted_iota(jnp.int32, (QCHUNK, QCHUNK), 1))

        c0, u0 = units[0]
        st_next = st_ref[pl.ds(0, carried), :] if carried else scores(start, n_keys(u0), c0, u0)
        for idx, (c, u) in enumerate(units):
            st = st_next
            if idx + 1 < len(units):
                c_next, u_next = units[idx + 1]
                st_next = scores(start, n_keys(u_next), c_next, u_next)
            elif carried:
                nxt = jnp.minimum(start + carried, s_len - carried)
                st_ref[pl.ds(0, carried), :] = scores(pl.multiple_of(nxt, QCHUNK), carried, c0, u0)
            nk = n_keys(u)
            cols = pl.ds(u * QCHUNK, QCHUNK)
            if masked:
                last = jnp.where(tri, st[nk - QCHUNK:], -jnp.inf)
                st = last if nk == QCHUNK else jnp.concatenate([st[:nk - QCHUNK], last], axis=0)
            vt = vt_ref[:, pl.ds(start, nk)]
            if online:
                m_old = m_ref[c, :, cols]
                m_new = jnp.maximum(m_old, jnp.max(st, axis=0, keepdims=True))
                p = jnp.exp2(st - m_new)
                alpha = jnp.exp2(m_old - m_new)
                l_ref[c, :, cols] = alpha * l_ref[c, :, cols] + jnp.sum(p, axis=0, keepdims=True)
                m_ref[c, :, cols] = m_new
                pv = jnp.dot(vt, p.astype(BF16), preferred_element_type=F32)
                acc_ref[c, :, cols] = alpha * acc_ref[c, :, cols] + pv
            else:
                p = jnp.exp2(st - m_ref[c, :, cols])
                l_ref[c, :, cols] += jnp.sum(p, axis=0, keepdims=True)
                acc_ref[c, :, cols] += jnp.dot(vt, p.astype(BF16), preferred_element_type=F32)

    def all_blocks(online, kb):
        assert tq % kb == 0
        diag = pl.multiple_of(qi * tq, tq)
        if online:
            m_ref[...] = jnp.full(m_ref.shape, -jnp.inf, F32)
        else:
            for r in range(0, tq, VT_CHUNK):
                kt = k_ref[pl.ds(pl.multiple_of(diag + r, VT_CHUNK), VT_CHUNK), :].T.astype(F32)
                for c in range(2):
                    qc = qh_ref[c, :, pl.ds(r, VT_CHUNK)].astype(F32)
                    m_ref[c, :, pl.ds(r, VT_CHUNK)] = jnp.sum(qc * kt, axis=0, keepdims=True)
        l_ref[...] = jnp.zeros(l_ref.shape, F32)
        acc_ref[...] = jnp.zeros(acc_ref.shape, F32)
        run_units(diag, lambda u: (u + 1) * QCHUNK, online=online, masked=True)
        st_ref[pl.ds(0, kb), :] = scores(0, kb, *units[0])

        def body(j, carry):
            run_units(pl.multiple_of(j * kb, kb), lambda u: kb, online=online, carried=kb)
            return carry

        lax.fori_loop(0, qi * (tq // kb), body, 0)

    lq1, lk1, lq2, lk2 = (lam_ref[pl.ds(r, 1), :] for r in range(4))
    lam = (jnp.exp(jnp.sum(lq1 * lk1, axis=-1, keepdims=True))
           - jnp.exp(jnp.sum(lq2 * lk2, axis=-1, keepdims=True)) + lambda_init)

    def finish():
        o = acc_ref[0] / l_ref[0] - lam * (acc_ref[1] / l_ref[1])
        ms = jnp.mean(o * o, axis=0, keepdims=True)
        on = o * lax.rsqrt(ms + EPS) * g_ref[...] * (1.0 - lambda_init)
        o_ref[...] = on.T.astype(BF16)
        return ms

    all_blocks(online=False, kb=tk)
    ms = finish()
    overflow_probe = jnp.sum(ms * 0.0) + jnp.sum(l_ref[...] * 0.0)

    @pl.when(jnp.logical_not(overflow_probe == 0.0))
    def _():
        all_blocks(online=True, kb=min(tk, tq, ONLINE_BLOCK))
        finish()


def _attention(z, lam_params, subln_g, layer, *, tq, tk, lambda_init):
    s = z.shape[1]
    kern = functools.partial(_attn_kernel, tq=tq, tk=tk, lambda_init=lambda_init)
    return pl.pallas_call(
        kern,
        grid=(N_HEADS, s // tq),
        in_specs=[
            _layer_block(lam_params, layer),
            _layer_block(subln_g, layer),
            pl.BlockSpec((None, tq, V_DIM), lambda h, i: (h, i, 0)),
            pl.BlockSpec((None, s, V_DIM), lambda h, i: (OFF_K // V_DIM + h, 0, 0)),
            pl.BlockSpec((None, s, V_DIM), lambda h, i: (OFF_V // V_DIM + h, 0, 0)),
        ],
        out_specs=pl.BlockSpec((tq, V_DIM), lambda h, i: (i, h)),
        out_shape=jax.ShapeDtypeStruct((s, ATTN_WIDTH), BF16),
        scratch_shapes=[
            pltpu.VMEM((V_DIM, s), BF16),
            pltpu.VMEM((2, V_DIM, tq), BF16),
            pltpu.VMEM((tk, QCHUNK), F32),
            pltpu.VMEM((2, 1, tq), F32),
            pltpu.VMEM((2, 1, tq), F32),
            pltpu.VMEM((2, V_DIM, tq), F32),
        ],
        compiler_params=pltpu.CompilerParams(
            dimension_semantics=("parallel", "arbitrary"),
            vmem_limit_bytes=VMEM_LIMIT_BYTES),
        name="diff_attn",
    )(lam_params, subln_g, z, z, z)


def _merge_kernel(x_ref, ya_ref, cb_ref, cc_ref, ch_ref, pu_ref, ga_ref, gb_ref, gc_ref,
                  cch_ref, chh_ref, puh_ref,
                  bg_ref, cw_ref, pw_ref, ps_ref, pa_ref, pb_ref, pc_ref, wo_ref, gp_ref,
                  o_ref, ext_ref, *, tm):
    i = pl.program_id(0)
    not_first = (i > 0).astype(F32)

    def wide(ref):
        return jnp.concatenate([ref[c] for c in range(ref.shape[0])], axis=-1).astype(F32)

    u = wide(cc_ref) * wide(ch_ref)
    ext_ref[pl.ds(0, HALO), :] = wide(cch_ref) * wide(chh_ref) * not_first
    ext_ref[pl.ds(HALO, tm), :] = u
    conv = u * cw_ref[pl.ds(CONV_K - 1, 1), :]
    for j in range(CONV_K - 1):
        back = CONV_K - 1 - j
        conv = conv + ext_ref[pl.ds(HALO - back, tm), :] * cw_ref[pl.ds(j, 1), :]
    y_conv = (wide(cb_ref) * conv).astype(BF16)

    pu = wide(pu_ref)
    ext_ref[pl.ds(0, HALO), :] = wide(puh_ref) * not_first
    ext_ref[pl.ds(HALO, tm), :] = pu
    t1 = (i * tm + 1 + lax.broadcasted_iota(jnp.int32, (tm, 1), 0)).astype(F32)
    groups = []
    for gi, w in enumerate(POOL_WINDOWS):
        cols = pl.ds(gi * POOL_GROUP, POOL_GROUP)
        ug = pu[:, gi * POOL_GROUP:(gi + 1) * POOL_GROUP]
        win = ug
        for back in range(1, w):
            win = win + ext_ref[pl.ds(HALO - back, tm), cols]
        dgrp = win / jnp.minimum(t1, float(w)) - ug
        groups.append(jnp.dot(dgrp.astype(BF16), pw_ref[gi], preferred_element_type=F32))
    y_pool = (jnp.concatenate(groups, axis=-1) * ps_ref[...]).astype(BF16)

    bg = bg_ref[...]
    d = x_ref.shape[-1]
    merged = (_sigmoid(wide(ga_ref) + bg[:, 0:d])
              * jnp.dot(ya_ref[...], pa_ref[...], preferred_element_type=F32))
    merged += (_sigmoid(wide(gb_ref) + bg[:, d:2 * d])
               * jnp.dot(y_conv, pb_ref[...], preferred_element_type=F32))
    merged += (_sigmoid(wide(gc_ref) + bg[:, 2 * d:3 * d])
               * jnp.dot(y_pool, pc_ref[...], preferred_element_type=F32))
    mix = jnp.dot(merged.astype(BF16), wo_ref[...], preferred_element_type=F32)
    o_ref[...] = x_ref[...] + _rms(mix, gp_ref[...])


def _merge(x, z, ya, bg, cw, pw, ps, pa, pb, pc, wo, gp, layer, *, tm):
    s, d = x.shape
    hb = tm // HALO

    def col(width, off):
        return pl.BlockSpec((width // LANES, tm, LANES), lambda i: (off // width, i, 0))

    def halo(off):
        return pl.BlockSpec((CONV_WIDTH // LANES, HALO, LANES),
                            lambda i: (off // CONV_WIDTH, jnp.maximum(i * hb - 1, 0), 0))

    def whole(a):
        return _layer_block(a, layer)

    kern = functools.partial(_merge_kernel, tm=tm)
    return pl.pallas_call(
        kern,
        grid=(s // tm,),
        in_specs=[
            pl.BlockSpec((tm, d), lambda i: (i, 0)),
            pl.BlockSpec((tm, ATTN_WIDTH), lambda i: (i, 0)),
            col(CONV_WIDTH, OFF_CB), col(CONV_WIDTH, OFF_CC), col(CONV_WIDTH, OFF_CH),
            col(POOL_WIDTH, OFF_POOL),
            col(d, OFF_GATE), col(d, OFF_GATE + d), col(d, OFF_GATE + 2 * d),
            halo(OFF_CC), halo(OFF_CH), halo(OFF_POOL),
            whole(bg), whole(cw), whole(pw), whole(ps), whole(pa), whole(pb), whole(pc),
            whole(wo), whole(gp),
        ],
        out_specs=pl.BlockSpec((tm, d), lambda i: (i, 0)),
        out_shape=jax.ShapeDtypeStruct((s, d), F32),
        scratch_shapes=[pltpu.VMEM((tm + HALO, CONV_WIDTH), F32)],
        compiler_params=pltpu.CompilerParams(
            dimension_semantics=("parallel",),
            vmem_limit_bytes=VMEM_LIMIT_BYTES),
        name="merge",
    )(x, ya, z, z, z, z, z, z, z, z, z, z, bg, cw, pw, ps, pa, pb, pc, wo, gp)


def _ffn_kernel(x_ref, gpre_ref, wg_ref, wu_ref, wd_ref, gpost_ref, o_ref, hn_ref, acc_ref):
    c = pl.program_id(1)

    @pl.when(c == 0)
    def _():
        hn_ref[...] = _rms(x_ref[...], gpre_ref[...]).astype(BF16)
        acc_ref[...] = jnp.zeros(acc_ref.shape, F32)

    hn = hn_ref[...]
    gate = jnp.dot(hn, wg_ref[...], preferred_element_type=F32)
    up = jnp.dot(hn, wu_ref[...], preferred_element_type=F32)
    hf = (gate * _sigmoid(gate) * up).astype(BF16)
    acc_ref[...] += jnp.dot(hf, wd_ref[...], preferred_element_type=F32)

    @pl.when(c == pl.num_programs(1) - 1)
    def _():
        o_ref[...] = x_ref[...] + _rms(acc_ref[...], gpost_ref[...])


def _ffn(x, gpre, wgu, wd, gpost, layer, *, tm, fc):
    s, d = x.shape
    d_ff = wd.shape[1]
    nc = d_ff // fc
    return pl.pallas_call(
        _ffn_kernel,
        grid=(s // tm, nc),
        in_specs=[
            pl.BlockSpec((tm, d), lambda i, c: (i, 0)),
            _layer_block(gpre, layer),
            _layer_block(wgu, layer, (d, fc), lambda i, c: (0, c)),
            _layer_block(wgu, layer, (d, fc), lambda i, c: (0, nc + c)),
            _layer_block(wd, layer, (fc, d), lambda i, c: (c, 0)),
            _layer_block(gpost, layer),
        ],
        out_specs=pl.BlockSpec((tm, d), lambda i, c: (i, 0)),
        out_shape=jax.ShapeDtypeStruct((s, d), F32),
        scratch_shapes=[pltpu.VMEM((tm, d), BF16), pltpu.VMEM((tm, d), F32)],
        compiler_params=pltpu.CompilerParams(
            dimension_semantics=("parallel", "arbitrary"),
            vmem_limit_bytes=VMEM_LIMIT_BYTES),
        name="ffn",
    )(x, gpre, wgu, wgu, wd, gpost)


def _tiles(s):
    pick = lambda pref: min(pref, s)
    return dict(proj_tm=pick(1024), proj_tn=2048, tq=pick(2048), tk=pick(2048),
                merge_tm=pick(512), ffn_tm=pick(1024))


def kernel(x, g_pre_mix, w_in, b_gate, lambda_q1, lambda_k1, lambda_q2, lambda_k2, subln_g,
           w_attn_proj, conv_w, w_conv_proj, pool_w, pool_scale, w_pool_proj, w_out,
           g_post_mix, g_pre_ffn, w_gate_up, w_down, g_post_ffn):
    b, s, d = x.shape
    depth = w_in.shape[0]
    d_ff = w_down.shape[1]
    t = _tiles(s)
    rows = lambda a: a.reshape(depth, 1, -1)
    bf = lambda a: a.astype(BF16)
    g_pre_mix, b_gate, pool_scale, g_post_mix, g_pre_ffn, g_post_ffn = map(
        rows, (g_pre_mix, b_gate, pool_scale, g_post_mix, g_pre_ffn, g_post_ffn))
    w_in, pool_w, w_attn_proj, w_conv_proj, w_pool_proj, w_out, w_gate_up, w_down = map(
        bf, (w_in, pool_w, w_attn_proj, w_conv_proj, w_pool_proj, w_out, w_gate_up, w_down))
    lam_params = jnp.stack([lambda_q1, lambda_k1, lambda_q2, lambda_k2], axis=1).astype(F32)
    subln_col = subln_g.reshape(depth, V_DIM, 1)
    outs = []
    for bi in range(b):
        h = x[bi]
        for l in range(depth):
            lambda_init = 0.8 - 0.6 * math.exp(-0.3 * l)
            z = _in_proj(h, g_pre_mix, w_in, l, tm=t["proj_tm"], tn=t["proj_tn"])
            ya = _attention(z, lam_params, subln_col, l, tq=t["tq"], tk=t["tk"], lambda_init=lambda_init)
            h = _merge(h, z, ya, b_gate, conv_w, pool_w, pool_scale, w_attn_proj, w_conv_proj,
                       w_pool_proj, w_out, g_post_mix, l, tm=t["merge_tm"])
            h = _ffn(h, g_pre_ffn, w_gate_up, w_down, g_post_ffn, l, tm=t["ffn_tm"], fc=d_ff // 2)
        outs.append(h)
    return jnp.stack(outs)
```

```python
import functools
import math

import jax
import jax.numpy as jnp
from jax import lax
from jax.experimental import pallas as pl
from jax.experimental.pallas import tpu as pltpu

F32 = jnp.float32
BF16 = jnp.bfloat16
LANES = 128

EPS = 1e-6
N_HEADS = 8
HEAD_DIM = 64
V_DIM = 2 * HEAD_DIM
ATTN_WIDTH = N_HEADS * V_DIM
CONV_WIDTH = 512
CONV_K = 3
POOL_WINDOWS = (2, 4, 8, 16)
POOL_GROUP = 128
POOL_WIDTH = POOL_GROUP * len(POOL_WINDOWS)
HALO = 16
QCHUNK = 256
VT_CHUNK = 512
LOG2_E = math.log2(math.e)
ONLINE_BLOCK = 1024

OFF_K = ATTN_WIDTH
OFF_V = 2 * ATTN_WIDTH
OFF_CB = 3 * ATTN_WIDTH
OFF_CC = OFF_CB + CONV_WIDTH
OFF_CH = OFF_CC + CONV_WIDTH
OFF_POOL = OFF_CH + CONV_WIDTH
OFF_GATE = OFF_POOL + POOL_WIDTH

VMEM_LIMIT_BYTES = 56 * 1024 * 1024


def _layer_block(a, layer, block=None, index=None):
    rest = a.shape[1:] if block is None else block
    if index is None:
        index = lambda *grid: (0,) * len(rest)
    return pl.BlockSpec((None,) + tuple(rest), lambda *grid: (layer,) + tuple(index(*grid)))


def _sigmoid(x):
    return 0.5 * jnp.tanh(0.5 * x) + 0.5


def _rms(x, g):
    return x * lax.rsqrt(jnp.mean(x * x, axis=-1, keepdims=True) + EPS) * g


def _inproj_kernel(x_ref, g_ref, w_ref, z_ref, xn_ref):
    @pl.when(pl.program_id(1) == 0)
    def _():
        xn_ref[...] = _rms(x_ref[...], g_ref[...]).astype(BF16)

    z = jnp.dot(xn_ref[...], w_ref[...], preferred_element_type=F32).astype(BF16)
    for c in range(z_ref.shape[0]):
        z_ref[c] = z[:, c * LANES:(c + 1) * LANES]


def _in_proj(x, g, w, layer, *, tm, tn):
    s, d = x.shape
    n = w.shape[-1]
    return pl.pallas_call(
        _inproj_kernel,
        grid=(s // tm, n // tn),
        in_specs=[
            pl.BlockSpec((tm, d), lambda i, j: (i, 0)),
            _layer_block(g, layer),
            _layer_block(w, layer, (d, tn), lambda i, j: (0, j)),
        ],
        out_specs=pl.BlockSpec((tn // LANES, tm, LANES), lambda i, j: (j, i, 0)),
        out_shape=jax.ShapeDtypeStruct((n // LANES, s, LANES), BF16),
        scratch_shapes=[pltpu.VMEM((tm, d), BF16)],
        compiler_params=pltpu.CompilerParams(
            dimension_semantics=("parallel", "arbitrary"),
            vmem_limit_bytes=VMEM_LIMIT_BYTES),
        name="in_proj",
    )(x, g, w)


def _attn_kernel(lam_ref, g_ref, q_ref, k_ref, v_ref, o_ref,
                 vt_ref, qh_ref, st_ref, m_ref, l_ref, acc_ref, *, tq, tk, lambda_init):
    qi = pl.program_id(1)
    s_len = k_ref.shape[0]

    @pl.when(qi == 0)
    def _():
        for r in range(0, s_len, VT_CHUNK):
            vt_ref[:, pl.ds(r, VT_CHUNK)] = v_ref[pl.ds(r, VT_CHUNK), :].T

    q = (q_ref[...].astype(F32) * (HEAD_DIM ** -0.5 * LOG2_E)).astype(BF16)
    qt = q.T
    feat = lax.broadcasted_iota(jnp.int32, (V_DIM, tq), 0)
    zero = jnp.zeros_like(qt)
    qh_ref[0] = jnp.where(feat < HEAD_DIM, qt, zero)
    qh_ref[1] = jnp.where(feat >= HEAD_DIM, qt, zero)

    units = [(c, u) for u in range(tq // QCHUNK) for c in range(2)]

    def scores(start, nk, c, u):
        kb = k_ref[pl.ds(start, nk), :]
        qc = qh_ref[c, :, pl.ds(u * QCHUNK, QCHUNK)]
        return jnp.dot(kb, qc, preferred_element_type=F32)

    def run_units(start, n_keys, *, online, masked=False, carried=0):
        if masked:
            tri = (lax.broadcasted_iota(jnp.int32, (QCHUNK, QCHUNK), 0)
                   <= lax.broadcasted_iota(jnp.int32, (QCHUNK, QCHUNK), 1))

        c0, u0 = units[0]
        st_next = st_ref[pl.ds(0, carried), :] if carried else scores(start, n_keys(u0), c0, u0)
        for idx, (c, u) in enumerate(units):
            st = st_next
            if idx + 1 < len(units):
                c_next, u_next = units[idx + 1]
                st_next = scores(start, n_keys(u_next), c_next, u_next)
            elif carried:
                nxt = jnp.minimum(start + carried, s_len - carried)
                st_ref[pl.ds(0, carried), :] = scores(pl.multiple_of(nxt, QCHUNK), carried, c0, u0)
            nk = n_keys(u)
            cols = pl.ds(u * QCHUNK, QCHUNK)
            if masked:
                last = jnp.where(tri, st[nk - QCHUNK:], -jnp.inf)
                st = last if nk == QCHUNK else jnp.concatenate([st[:nk - QCHUNK], last], axis=0)
            vt = vt_ref[:, pl.ds(start, nk)]
            if online:
                m_old = m_ref[c, :, cols]
                m_new = jnp.maximum(m_old, jnp.max(st, axis=0, keepdims=True))
                p = jnp.exp2(st - m_new)
                alpha = jnp.exp2(m_old - m_new)
                l_ref[c, :, cols] = alpha * l_ref[c, :, cols] + jnp.sum(p, axis=0, keepdims=True)
                m_ref[c, :, cols] = m_new
                pv = jnp.dot(vt, p.astype(BF16), preferred_element_type=F32)
                acc_ref[c, :, cols] = alpha * acc_ref[c, :, cols] + pv
            else:
                p = jnp.exp2(st - m_ref[c, :, cols])
                l_ref[c, :, cols] += jnp.sum(p, axis=0, keepdims=True)
                acc_ref[c, :, cols] += jnp.dot(vt, p.astype(BF16), preferred_element_type=F32)

    def all_blocks(online, kb):
        assert tq % kb == 0
        diag = pl.multiple_of(qi * tq, tq)
        if online:
            m_ref[...] = jnp.full(m_ref.shape, -jnp.inf, F32)
        else:
            for r in range(0, tq, VT_CHUNK):
                kt = k_ref[pl.ds(pl.multiple_of(diag + r, VT_CHUNK), VT_CHUNK), :].T.astype(F32)
                for c in range(2):
                    qc = qh_ref[c, :, pl.ds(r, VT_CHUNK)].astype(F32)
                    m_ref[c, :, pl.ds(r, VT_CHUNK)] = jnp.sum(qc * kt, axis=0, keepdims=True)
        l_ref[...] = jnp.zeros(l_ref.shape, F32)
        acc_ref[...] = jnp.zeros(acc_ref.shape, F32)
        run_units(diag, lambda u: (u + 1) * QCHUNK, online=online, masked=True)
        st_ref[pl.ds(0, kb), :] = scores(0, kb, *units[0])

        def body(j, carry):
            run_units(pl.multiple_of(j * kb, kb), lambda u: kb, online=online, carried=kb)
            return carry

        lax.fori_loop(0, qi * (tq // kb), body, 0)

    lq1, lk1, lq2, lk2 = (lam_ref[pl.ds(r, 1), :] for r in range(4))
    lam = (jnp.exp(jnp.sum(lq1 * lk1, axis=-1, keepdims=True))
           - jnp.exp(jnp.sum(lq2 * lk2, axis=-1, keepdims=True)) + lambda_init)

    def finish():
        o = acc_ref[0] / l_ref[0] - lam * (acc_ref[1] / l_ref[1])
        ms = jnp.mean(o * o, axis=0, keepdims=True)
        on = o * lax.rsqrt(ms + EPS) * g_ref[...] * (1.0 - lambda_init)
        o_ref[...] = on.T.astype(BF16)
        return ms

    all_blocks(online=False, kb=tk)
    ms = finish()
    overflow_probe = jnp.sum(ms * 0.0) + jnp.sum(l_ref[...] * 0.0)

    @pl.when(jnp.logical_not(overflow_probe == 0.0))
    def _():
        all_blocks(online=True, kb=min(tk, tq, ONLINE_BLOCK))
        finish()


def _attention(z, lam_params, subln_g, layer, *, tq, tk, lambda_init):
    s = z.shape[1]
    kern = functools.partial(_attn_kernel, tq=tq, tk=tk, lambda_init=lambda_init)
    return pl.pallas_call(
        kern,
        grid=(N_HEADS, s // tq),
        in_specs=[
            _layer_block(lam_params, layer),
            _layer_block(subln_g, layer),
            pl.BlockSpec((None, tq, V_DIM), lambda h, i: (h, i, 0)),
            pl.BlockSpec((None, s, V_DIM), lambda h, i: (OFF_K // V_DIM + h, 0, 0)),
            pl.BlockSpec((None, s, V_DIM), lambda h, i: (OFF_V // V_DIM + h, 0, 0)),
        ],
        out_specs=pl.BlockSpec((tq, V_DIM), lambda h, i: (i, h)),
        out_shape=jax.ShapeDtypeStruct((s, ATTN_WIDTH), BF16),
        scratch_shapes=[
            pltpu.VMEM((V_DIM, s), BF16),
            pltpu.VMEM((2, V_DIM, tq), BF16),
            pltpu.VMEM((tk, QCHUNK), F32),
            pltpu.VMEM((2, 1, tq), F32),
            pltpu.VMEM((2, 1, tq), F32),
            pltpu.VMEM((2, V_DIM, tq), F32),
        ],
        compiler_params=pltpu.CompilerParams(
            dimension_semantics=("parallel", "arbitrary"),
            vmem_limit_bytes=VMEM_LIMIT_BYTES),
        name="diff_attn",
    )(lam_params, subln_g, z, z, z)


def _merge_kernel(x_ref, ya_ref, cb_ref, cc_ref, ch_ref, pu_ref, ga_ref, gb_ref, gc_ref,
                  cch_ref, chh_ref, puh_ref,
                  bg_ref, cw_ref, pw_ref, ps_ref, pa_ref, pb_ref, pc_ref, wo_ref, gp_ref,
                  o_ref, ext_ref, *, tm):
    i = pl.program_id(0)
    not_first = (i > 0).astype(F32)

    def wide(ref):
        return jnp.concatenate([ref[c] for c in range(ref.shape[0])], axis=-1).astype(F32)

    u = wide(cc_ref) * wide(ch_ref)
    ext_ref[pl.ds(0, HALO), :] = wide(cch_ref) * wide(chh_ref) * not_first
    ext_ref[pl.ds(HALO, tm), :] = u
    conv = u * cw_ref[pl.ds(CONV_K - 1, 1), :]
    for j in range(CONV_K - 1):
        back = CONV_K - 1 - j
        conv = conv + ext_ref[pl.ds(HALO - back, tm), :] * cw_ref[pl.ds(j, 1), :]
    y_conv = (wide(cb_ref) * conv).astype(BF16)

    pu = wide(pu_ref)
    ext_ref[pl.ds(0, HALO), :] = wide(puh_ref) * not_first
    ext_ref[pl.ds(HALO, tm), :] = pu
    t1 = (i * tm + 1 + lax.broadcasted_iota(jnp.int32, (tm, 1), 0)).astype(F32)
    groups = []
    for gi, w in enumerate(POOL_WINDOWS):
        cols = pl.ds(gi * POOL_GROUP, POOL_GROUP)
        ug = pu[:, gi * POOL_GROUP:(gi + 1) * POOL_GROUP]
        win = ug
        for back in range(1, w):
            win = win + ext_ref[pl.ds(HALO - back, tm), cols]
        dgrp = win / jnp.minimum(t1, float(w)) - ug
        groups.append(jnp.dot(dgrp.astype(BF16), pw_ref[gi], preferred_element_type=F32))
    y_pool = (jnp.concatenate(groups, axis=-1) * ps_ref[...]).astype(BF16)

    bg = bg_ref[...]
    d = x_ref.shape[-1]
    merged = (_sigmoid(wide(ga_ref) + bg[:, 0:d])
              * jnp.dot(ya_ref[...], pa_ref[...], preferred_element_type=F32))
    merged += (_sigmoid(wide(gb_ref) + bg[:, d:2 * d])
               * jnp.dot(y_conv, pb_ref[...], preferred_element_type=F32))
    merged += (_sigmoid(wide(gc_ref) + bg[:, 2 * d:3 * d])
               * jnp.dot(y_pool, pc_ref[...], preferred_element_type=F32))
    mix = jnp.dot(merged.astype(BF16), wo_ref[...], preferred_element_type=F32)
    o_ref[...] = x_ref[...] + _rms(mix, gp_ref[...])


def _merge(x, z, ya, bg, cw, pw, ps, pa, pb, pc, wo, gp, layer, *, tm):
    s, d = x.shape
    hb = tm // HALO

    def col(width, off):
        return pl.BlockSpec((width // LANES, tm, LANES), lambda i: (off // width, i, 0))

    def halo(off):
        return pl.BlockSpec((CONV_WIDTH // LANES, HALO, LANES),
                            lambda i: (off // CONV_WIDTH, jnp.maximum(i * hb - 1, 0), 0))

    def whole(a):
        return _layer_block(a, layer)

    kern = functools.partial(_merge_kernel, tm=tm)
    return pl.pallas_call(
        kern,
        grid=(s // tm,),
        in_specs=[
            pl.BlockSpec((tm, d), lambda i: (i, 0)),
            pl.BlockSpec((tm, ATTN_WIDTH), lambda i: (i, 0)),
            col(CONV_WIDTH, OFF_CB), col(CONV_WIDTH, OFF_CC), col(CONV_WIDTH, OFF_CH),
            col(POOL_WIDTH, OFF_POOL),
            col(d, OFF_GATE), col(d, OFF_GATE + d), col(d, OFF_GATE + 2 * d),
            halo(OFF_CC), halo(OFF_CH), halo(OFF_POOL),
            whole(bg), whole(cw), whole(pw), whole(ps), whole(pa), whole(pb), whole(pc),
            whole(wo), whole(gp),
        ],
        out_specs=pl.BlockSpec((tm, d), lambda i: (i, 0)),
        out_shape=jax.ShapeDtypeStruct((s, d), F32),
        scratch_shapes=[pltpu.VMEM((tm + HALO, CONV_WIDTH), F32)],
        compiler_params=pltpu.CompilerParams(
            dimension_semantics=("parallel",),
            vmem_limit_bytes=VMEM_LIMIT_BYTES),
        name="merge",
    )(x, ya, z, z, z, z, z, z, z, z, z, z, bg, cw, pw, ps, pa, pb, pc, wo, gp)


def _ffn_kernel(x_ref, gpre_ref, wg_ref, wu_ref, wd_ref, gpost_ref, o_ref, hn_ref, acc_ref):
    c = pl.program_id(1)

    @pl.when(c == 0)
    def _():
        hn_ref[...] = _rms(x_ref[...], gpre_ref[...]).astype(BF16)
        acc_ref[...] = jnp.zeros(acc_ref.shape, F32)

    hn = hn_ref[...]
    gate = jnp.dot(hn, wg_ref[...], preferred_element_type=F32)
    up = jnp.dot(hn, wu_ref[...], preferred_element_type=F32)
    hf = (gate * _sigmoid(gate) * up).astype(BF16)
    acc_ref[...] += jnp.dot(hf, wd_ref[...], preferred_element_type=F32)

    @pl.when(c == pl.num_programs(1) - 1)
    def _():
        o_ref[...] = x_ref[...] + _rms(acc_ref[...], gpost_ref[...])


def _ffn(x, gpre, wgu, wd, gpost, layer, *, tm, fc):
    s, d = x.shape
    d_ff = wd.shape[1]
    nc = d_ff // fc
    return pl.pallas_call(
        _ffn_kernel,
        grid=(s // tm, nc),
        in_specs=[
            pl.BlockSpec((tm, d), lambda i, c: (i, 0)),
            _layer_block(gpre, layer),
            _layer_block(wgu, layer, (d, fc), lambda i, c: (0, c)),
            _layer_block(wgu, layer, (d, fc), lambda i, c: (0, nc + c)),
            _layer_block(wd, layer, (fc, d), lambda i, c: (c, 0)),
            _layer_block(gpost, layer),
        ],
        out_specs=pl.BlockSpec((tm, d), lambda i, c: (i, 0)),
        out_shape=jax.ShapeDtypeStruct((s, d), F32),
        scratch_shapes=[pltpu.VMEM((tm, d), BF16), pltpu.VMEM((tm, d), F32)],
        compiler_params=pltpu.CompilerParams(
            dimension_semantics=("parallel", "arbitrary"),
            vmem_limit_bytes=VMEM_LIMIT_BYTES),
        name="ffn",
    )(x, gpre, wgu, wgu, wd, gpost)


def _tiles(s):
    pick = lambda pref: min(pref, s)
    return dict(proj_tm=pick(1024), proj_tn=2048, tq=pick(4096), tk=pick(2048),
                merge_tm=pick(512), ffn_tm=pick(1024))


def kernel(x, g_pre_mix, w_in, b_gate, lambda_q1, lambda_k1, lambda_q2, lambda_k2, subln_g,
           w_attn_proj, conv_w, w_conv_proj, pool_w, pool_scale, w_pool_proj, w_out,
           g_post_mix, g_pre_ffn, w_gate_up, w_down, g_post_ffn):
    b, s, d = x.shape
    depth = w_in.shape[0]
    d_ff = w_down.shape[1]
    t = _tiles(s)
    rows = lambda a: a.reshape(depth, 1, -1)
    bf = lambda a: a.astype(BF16)
    g_pre_mix, b_gate, pool_scale, g_post_mix, g_pre_ffn, g_post_ffn = map(
        rows, (g_pre_mix, b_gate, pool_scale, g_post_mix, g_pre_ffn, g_post_ffn))
    w_in, pool_w, w_attn_proj, w_conv_proj, w_pool_proj, w_out, w_gate_up, w_down = map(
        bf, (w_in, pool_w, w_attn_proj, w_conv_proj, w_pool_proj, w_out, w_gate_up, w_down))
    lam_params = jnp.stack([lambda_q1, lambda_k1, lambda_q2, lambda_k2], axis=1).astype(F32)
    subln_col = subln_g.reshape(depth, V_DIM, 1)
    outs = []
    for bi in range(b):
        h = x[bi]
        for l in range(depth):
            lambda_init = 0.8 - 0.6 * math.exp(-0.3 * l)
            z = _in_proj(h, g_pre_mix, w_in, l, tm=t["proj_tm"], tn=t["proj_tn"])
            ya = _attention(z, lam_params, subln_col, l, tq=t["tq"], tk=t["tk"], lambda_init=lambda_init)
            h = _merge(h, z, ya, b_gate, conv_w, pool_w, pool_scale, w_attn_proj, w_conv_proj,
                       w_pool_proj, w_out, g_post_mix, l, tm=t["merge_tm"])
            h = _ffn(h, g_pre_ffn, w_gate_up, w_down, g_post_ffn, l, tm=t["ffn_tm"], fc=d_ff // 2)
        outs.append(h)
    return jnp.stack(outs)
```
